```python
import math
import jax, jax.numpy as jnp
from jax import lax
import numpy as np

D_MODEL = 1024
BATCH = 4
SEQ = 4096
DEPTH = 1

RET_HEADS = 8
RET_QK_DIM = 64
RET_V_DIM = 128
RET_CHUNK = 128
ROPE_BASE = 10000.0

MOBA_HEADS = 8
MOBA_HEAD_DIM = 64
MOBA_BLOCK = 256
MOBA_TOPK = 3
MOBA_Q_CHUNK = 32

REL_BUCKETS = 32
REL_MAX_DIST = 2048

PEER_HEADS = 8
PEER_N_KEYS = 128
PEER_N_EXPERTS = PEER_N_KEYS * PEER_N_KEYS
PEER_QUERY_DIM = 256
PEER_TOPK = 16
PEER_TOKEN_CHUNK = 128

NORM_EPS = 1e-6
NEG_INF = -1e30

RET_QK_W = RET_HEADS * RET_QK_DIM
RET_V_W = RET_HEADS * RET_V_DIM
MOBA_W = MOBA_HEADS * MOBA_HEAD_DIM
IN_SIZES = (RET_QK_W, RET_QK_W, RET_V_W, RET_V_W, MOBA_W, MOBA_W, MOBA_W, D_MODEL, D_MODEL)
IN_WIDTH = sum(IN_SIZES)

kernel_name = "hybrid_retention_moba_peer_layer"


def rmsnorm(x, g):
    xf = x.astype(jnp.float32)
    y = xf * lax.rsqrt(jnp.mean(xf * xf, axis=-1, keepdims=True) + NORM_EPS)
    return (y * g.astype(jnp.float32)).astype(x.dtype)


def to_heads(t, n_heads):
    b, s, _ = t.shape
    return t.reshape(b, s, n_heads, -1).transpose(0, 2, 1, 3)


def from_heads(t):
    b, h, s, d = t.shape
    return t.transpose(0, 2, 1, 3).reshape(b, s, h * d)


def rotary(x, pos):
    half = x.shape[-1] // 2
    inv = ROPE_BASE ** (-jnp.arange(half, dtype=jnp.float32) / half)
    ang = pos.astype(jnp.float32)[:, None] * inv[None, :]
    cos, sin = jnp.cos(ang), jnp.sin(ang)
    x1, x2 = x[..., :half], x[..., half:]
    return jnp.concatenate([x1 * cos - x2 * sin, x1 * sin + x2 * cos], axis=-1).astype(x.dtype)


def retention(q, k, v):
    B, H, S, dk = q.shape
    dv = v.shape[-1]
    C = RET_CHUNK
    n = S // C
    log_g = jnp.log1p(-jnp.exp2(-5.0 - jnp.arange(H, dtype=jnp.float32)))
    idx = jnp.arange(C, dtype=jnp.float32)
    rel = idx[:, None] - idx[None, :]
    decay_in = jnp.where(rel[None] >= 0, jnp.exp(jnp.maximum(rel, 0.0)[None] * log_g[:, None, None]), 0.0)
    qc = q.reshape(B, H, n, C, dk)
    kc = k.reshape(B, H, n, C, dk)
    vc = v.reshape(B, H, n, C, dv)
    scores = jnp.einsum('bhnid,bhnjd->bhnij', qc, kc) * decay_in[None, :, None]
    o_in = jnp.einsum('bhnij,bhnje->bhnie', scores, vc)
    k_dec = kc * jnp.exp((C - 1 - idx)[None, :] * log_g[:, None])[None, :, None, :, None]
    kv = jnp.einsum('bhncd,bhnce->bhnde', k_dec, vc)
    chunk_decay = jnp.exp(C * log_g)[None, :, None, None]

    def step(state, kv_n):
        return state * chunk_decay + kv_n, state

    state0 = jnp.zeros((B, H, dk, dv), kv.dtype)
    _, prev = lax.scan(step, state0, jnp.moveaxis(kv, 2, 0))
    prev = jnp.moveaxis(prev, 0, 2)
    q_dec = jnp.exp((idx + 1.0)[None, :] * log_g[:, None])
    o_cross = jnp.einsum('bhncd,bhnde->bhnce', qc, prev) * q_dec[None, :, None, :, None]
    o = (o_in + o_cross).reshape(B, H, S, dv).astype(jnp.float32)
    mu = jnp.mean(o, axis=-1, keepdims=True)
    var = jnp.mean(jnp.square(o - mu), axis=-1, keepdims=True)
    return ((o - mu) * lax.rsqrt(var + NORM_EPS)).astype(q.dtype)


def head_rmsnorm(t, g):
    tf = t.astype(jnp.float32)
    y = tf * lax.rsqrt(jnp.mean(tf * tf, axis=-1, keepdims=True) + NORM_EPS)
    return (y * g.astype(jnp.float32)).astype(t.dtype)


def rel_bucket(dist):
    max_exact = REL_BUCKETS // 2
    n = jnp.maximum(dist, 0)
    nf = jnp.maximum(n, 1).astype(jnp.float32)
    large = max_exact + (jnp.log(nf / max_exact) / math.log(REL_MAX_DIST / max_exact)
                         * (REL_BUCKETS - max_exact)).astype(jnp.int32)
    large = jnp.minimum(large, REL_BUCKETS - 1)
    return jnp.where(n < max_exact, n, large)


def moba_attention(q, k, v, rel_bias):
    B, H, S, dh = q.shape
    L = MOBA_BLOCK
    n_blk = -(-S // L)
    s_pad = n_blk * L
    pad = ((0, 0), (0, 0), (0, s_pad - S), (0, 0))
    k_pad = jnp.pad(k, pad)
    v_pad = jnp.pad(v, pad)
    k_blocks = k_pad.reshape(B, H, n_blk, L, dh)
    v_blocks = v_pad.reshape(B, H, n_blk, L, dh)
    k_mean = jnp.mean(k_blocks.astype(jnp.float32), axis=3)
    n_sel = min(MOBA_TOPK, n_blk)
    Qc = MOBA_Q_CHUNK
    n_qc = S // Qc
    q_chunks = jnp.moveaxis(q.reshape(B, H, n_qc, Qc, dh), 2, 0)
    bias_hb = rel_bias.T.astype(jnp.float32)
    b_idx = jnp.arange(B)[:, None, None, None]
    h_idx = jnp.arange(H)[None, :, None, None]
    h_idx5 = jnp.arange(H)[None, :, None, None, None]
    offs = jnp.arange(L)
    blk_ids = jnp.arange(n_blk)

    def one_chunk(args):
        qc, c = args
        q_start = c * Qc
        blk = q_start // L
        qpos = q_start + jnp.arange(Qc)
        gate = jnp.einsum('bhqd,bhnd->bhqn', qc.astype(jnp.float32), k_mean)
        gate = jnp.where(blk_ids < blk, gate, NEG_INF)
        _, sel = lax.top_k(gate, n_sel)
        valid = sel < blk
        k_sel = k_blocks[b_idx, h_idx, sel]
        v_sel = v_blocks[b_idx, h_idx, sel]
        kpos_sel = sel[..., None] * L + offs
        bias_sel = bias_hb[h_idx5, rel_bucket(qpos[None, None, :, None, None] - kpos_sel)]
        logit_sel = jnp.einsum('bhqd,bhqnkd->bhqnk', qc, k_sel).astype(jnp.float32) + bias_sel
        logit_sel = jnp.where(valid[..., None], logit_sel, NEG_INF).reshape(B, H, Qc, n_sel * L)
        k_own = lax.dynamic_slice_in_dim(k_pad, blk * L, L, axis=2)
        v_own = lax.dynamic_slice_in_dim(v_pad, blk * L, L, axis=2)
        dist_own = qpos[:, None] - (blk * L + offs)[None, :]
        bias_own = bias_hb[:, rel_bucket(dist_own)]
        logit_own = jnp.einsum('bhqd,bhkd->bhqk', qc, k_own).astype(jnp.float32) + bias_own[None]
        logit_own = jnp.where(dist_own >= 0, logit_own, NEG_INF)
        p = jax.nn.softmax(jnp.concatenate([logit_sel, logit_own], axis=-1), axis=-1)
        p_sel = p[..., :n_sel * L].reshape(B, H, Qc, n_sel, L).astype(v.dtype)
        p_own = p[..., n_sel * L:].astype(v.dtype)
        return (jnp.einsum('bhqnk,bhqnkd->bhqd', p_sel, v_sel)
                + jnp.einsum('bhqk,bhkd->bhqd', p_own, v_own))

    out = lax.map(one_chunk, (q_chunks, jnp.arange(n_qc)))
    return jnp.moveaxis(out, 0, 2).reshape(B, H, S, dh)


def peer(h, w_q, sub_keys, u_tab, v_tab):
    B, S, D = h.shape
    T = B * S
    ht = h.reshape(T, D)
    half = PEER_QUERY_DIM // 2
    q = (ht @ w_q).reshape(T, PEER_HEADS, 2, half)
    s = jnp.einsum('thcd,hcnd->thcn', q, sub_keys).astype(jnp.float32)
    s1, i1 = lax.top_k(s[:, :, 0], PEER_TOPK)
    s2, i2 = lax.top_k(s[:, :, 1], PEER_TOPK)
    cand_s = (s1[..., :, None] + s2[..., None, :]).reshape(T, PEER_HEADS, PEER_TOPK * PEER_TOPK)
    cand_i = (i1[..., :, None] * PEER_N_KEYS + i2[..., None, :]).reshape(T, PEER_HEADS, PEER_TOPK * PEER_TOPK)
    top_s, pos = lax.top_k(cand_s, PEER_TOPK)
    experts = jnp.take_along_axis(cand_i, pos, axis=-1)
    gates = jax.nn.softmax(top_s, axis=-1).astype(h.dtype)
    Pc = PEER_TOKEN_CHUNK
    n_c = T // Pc

    def one_chunk(args):
        hc, ec, gc = args
        u = u_tab[ec]
        a = jax.nn.gelu(jnp.einsum('td,thkd->thk', hc, u), approximate=False)
        return jnp.einsum('thk,thkd->td', gc * a, v_tab[ec])

    out = lax.map(one_chunk, (ht.reshape(n_c, Pc, D),
                              experts.reshape(n_c, Pc, PEER_HEADS, PEER_TOPK),
                              gates.reshape(n_c, Pc, PEER_HEADS, PEER_TOPK)))
    return out.reshape(B, S, D)


def setup_inputs(seed: int = 0) -> dict:
    key = jax.random.key(seed)
    ks = jax.random.split(key, 14)
    f32 = jnp.float32
    nrm = lambda k, shape, scale: jax.random.normal(k, shape, f32) * scale
    return {
        "x": nrm(ks[0], (BATCH, SEQ, D_MODEL), 1.0),
        "mix_norm_g": 1.0 + nrm(ks[1], (DEPTH, D_MODEL), 0.1),
        "w_in": nrm(ks[2], (DEPTH, D_MODEL, IN_WIDTH), D_MODEL ** -0.5),
        "ret_w_branch": nrm(ks[3], (DEPTH, RET_V_W, D_MODEL), RET_V_W ** -0.5),
        "moba_q_gain": 1.0 + nrm(ks[4], (DEPTH, MOBA_HEAD_DIM), 0.1),
        "moba_k_gain": 1.0 + nrm(ks[5], (DEPTH, MOBA_HEAD_DIM), 0.1),
        "moba_w_branch": nrm(ks[6], (DEPTH, MOBA_W, D_MODEL), MOBA_W ** -0.5),
        "rel_bias": nrm(ks[7], (REL_BUCKETS, MOBA_HEADS), 0.5),
        "w_out": nrm(ks[8], (DEPTH, D_MODEL, D_MODEL), D_MODEL ** -0.5),
        "ffn_norm_g": 1.0 + nrm(ks[9], (DEPTH, D_MODEL), 0.1),
        "peer_w_q": nrm(ks[10], (DEPTH, D_MODEL, PEER_HEADS * PEER_QUERY_DIM), D_MODEL ** -0.5),
        "peer_sub_keys": nrm(ks[11], (DEPTH, PEER_HEADS, 2, PEER_N_KEYS, PEER_QUERY_DIM // 2), (PEER_QUERY_DIM // 2) ** -0.5),
        "peer_u": nrm(ks[12], (DEPTH, PEER_N_EXPERTS, D_MODEL), D_MODEL ** -0.5),
        "peer_v": nrm(ks[13], (DEPTH, PEER_N_EXPERTS, D_MODEL), D_MODEL ** -0.5),
    }


def reference(x, mix_norm_g, w_in, ret_w_branch, moba_q_gain, moba_k_gain, moba_w_branch,
              rel_bias, w_out, ffn_norm_g, peer_w_q, peer_sub_keys, peer_u, peer_v):
    B, S, D = x.shape
    pos = jnp.arange(S)
    offsets = tuple(int(o) for o in np.cumsum(IN_SIZES)[:-1])
    for l in range(DEPTH):
        h = rmsnorm(x, mix_norm_g[l])
        proj = h @ w_in[l]
        rq, rk, rv, rg, mq, mk, mv, ga, gb = jnp.split(proj, offsets, axis=-1)
        rq = rotary(to_heads(rq, RET_HEADS), pos)
        rk = rotary(to_heads(rk, RET_HEADS), pos) * (RET_QK_DIM ** -0.5)
        ret = retention(rq, rk, to_heads(rv, RET_HEADS))
        y_a = (from_heads(ret) * jax.nn.silu(rg)) @ ret_w_branch[l]
        mq = head_rmsnorm(to_heads(mq, MOBA_HEADS), moba_q_gain[l]) * (MOBA_HEAD_DIM ** -0.5)
        mk = head_rmsnorm(to_heads(mk, MOBA_HEADS), moba_k_gain[l])
        att = moba_attention(mq, mk, to_heads(mv, MOBA_HEADS), rel_bias)
        y_b = from_heads(att) @ moba_w_branch[l]
        merged = jax.nn.sigmoid(ga) * y_a + jax.nn.sigmoid(gb) * y_b
        x = x + merged @ w_out[l]
        h2 = rmsnorm(x, ffn_norm_g[l])
        x = x + peer(h2, peer_w_q[l], peer_sub_keys[l], peer_u[l], peer_v[l])
    return x
```

```python
import functools
import math

import numpy as np
import jax
import jax.numpy as jnp
from jax import lax
from jax.experimental import pallas as pl
from jax.experimental.pallas import tpu as pltpu

F32 = jnp.float32
BF16 = jnp.bfloat16

D_MODEL = 1024
RET_HEADS = 8
RET_QK_DIM = 64
RET_V_DIM = 128
ROPE_BASE = 10000.0
MOBA_HEADS = 8
MOBA_HEAD_DIM = 64
MOBA_BLOCK = 256
MOBA_TOPK = 3
REL_BUCKETS = 32
REL_MAX_DIST = 2048
PEER_HEADS = 8
PEER_N_KEYS = 128
PEER_N_EXPERTS = PEER_N_KEYS * PEER_N_KEYS
PEER_QUERY_DIM = 256
PEER_TOPK = 16
NORM_EPS = 1e-6
NEG_INF = -1e30
BIG = 3.0e38

LANES = 128
VMEM_LIMIT = 56 * 1024 * 1024

COL_GA, COL_GB = 0, 1024
COL_RQ, COL_RK, COL_RV, COL_RG = 2048, 2560, 3072, 4096
COL_MQ, COL_MK, COL_MV = 5120, 5632, 6144
IN_WIDTH = 6656
ORIG_GATE_START = 4608

RET_CHUNK = 256
N_BIAS_TILES = 8


def _cparams(sem):
    return pltpu.CompilerParams(dimension_semantics=sem, vmem_limit_bytes=VMEM_LIMIT)


def _proj_kernel(x_ref, g_ref, w_ref, o_ref, h_ref):
    @pl.when(pl.program_id(1) == 0)
    def _():
        x = x_ref[...]
        ms = jnp.mean(x * x, axis=-1, keepdims=True)
        h_ref[...] = (x * lax.rsqrt(ms + NORM_EPS) * g_ref[...]).astype(BF16)

    o_ref[...] = jnp.dot(h_ref[...], w_ref[...], preferred_element_type=F32)


def _proj(x2d, g, w_bf16, tm=512, tn=1664):
    T, D = x2d.shape
    N = w_bf16.shape[1]
    return pl.pallas_call(
        _proj_kernel,
        grid=(T // tm, N // tn),
        in_specs=[
            pl.BlockSpec((tm, D), lambda i, j: (i, 0)),
            pl.BlockSpec((1, D), lambda i, j: (0, 0)),
            pl.BlockSpec((D, tn), lambda i, j: (0, j)),
        ],
        out_specs=pl.BlockSpec((tm, tn), lambda i, j: (i, j)),
        out_shape=jax.ShapeDtypeStruct((T, N), F32),
        scratch_shapes=[pltpu.VMEM((tm, D), BF16)],
        compiler_params=_cparams(("parallel", "arbitrary")),
        name="proj",
    )(x2d, g, w_bf16)


def _rope(x, cos, sin_signed, first_half):
    swapped = jnp.where(first_half, pltpu.roll(x, LANES - 32, 1), pltpu.roll(x, 32, 1))
    return x * cos + swapped * sin_signed


def _retention_kernel(q_ref, k_ref, v_ref, rg_ref, cos_ref, sin_ref, dmat_ref,
                      kdec_ref, qdec_ref, cdm_ref, bmask_ref, o_ref, state_ref):
    @pl.when(pl.program_id(2) == 0)
    def _():
        state_ref[...] = jnp.zeros_like(state_ref)

    C = q_ref.shape[1]
    lane = lax.broadcasted_iota(jnp.int32, (C, LANES), 1)
    first_half = (lane % RET_QK_DIM) < (RET_QK_DIM // 2)
    cos = cos_ref[...]
    sin = sin_ref[...]
    qr = _rope(q_ref[0], cos, sin, first_half)
    kr = _rope(k_ref[0], cos, sin, first_half) * (RET_QK_DIM ** -0.5)
    kr_b = kr.astype(BF16)
    v_b = v_ref[0].astype(BF16)
    state = state_ref[...]

    o_cross = jnp.dot(qr.astype(BF16), state.astype(BF16),
                      preferred_element_type=F32) * qdec_ref[0]
    rg = rg_ref[0]
    for h in range(2):
        qh = jnp.where((lane // RET_QK_DIM) == h, qr, 0.0).astype(BF16)
        s = lax.dot_general(qh, kr_b, (((1,), (1,)), ((), ())),
                            preferred_element_type=F32)
        s = s * dmat_ref[h]
        sl = slice(h * RET_V_DIM, (h + 1) * RET_V_DIM)
        o = jnp.dot(s.astype(BF16), v_b[:, sl], preferred_element_type=F32) + o_cross[:, sl]
        mu = jnp.mean(o, axis=-1, keepdims=True)
        d = o - mu
        var = jnp.mean(d * d, axis=-1, keepdims=True)
        y = d * lax.rsqrt(var + NORM_EPS)
        g = rg[:, sl]
        o_ref[0, :, sl] = (y * (g * jax.nn.sigmoid(g))).astype(BF16)

    kd = (kr * kdec_ref[0]).astype(BF16)
    kv = lax.dot_general(kd, v_b, (((0,), (0,)), ((), ())), preferred_element_type=F32)
    state_ref[...] = state * cdm_ref[0] + kv * bmask_ref[...]


def _retention_tables(S, C):
    H = RET_HEADS
    log_g = jnp.log1p(-jnp.exp2(-5.0 - jnp.arange(H, dtype=F32)))
    idx = jnp.arange(C, dtype=F32)
    rel = idx[:, None] - idx[None, :]
    dmat = jnp.where(rel[None] >= 0,
                     jnp.exp(jnp.maximum(rel, 0.0)[None] * log_g[:, None, None]), 0.0)
    kdec = jnp.exp((C - 1 - idx)[None, :] * log_g[:, None])
    qdec = jnp.exp((idx + 1.0)[None, :] * log_g[:, None])
    cd = jnp.exp(C * log_g)
    kdec_pair = jnp.repeat(kdec.reshape(H // 2, 2, C).transpose(0, 2, 1), RET_QK_DIM, axis=2)
    qdec_pair = jnp.repeat(qdec.reshape(H // 2, 2, C).transpose(0, 2, 1), RET_V_DIM, axis=2)
    rows = np.arange(2 * RET_QK_DIM)[:, None] // RET_QK_DIM
    cols = np.arange(2 * RET_V_DIM)[None, :] // RET_V_DIM
    bmask = jnp.asarray((rows == cols).astype(np.float32))
    cd_rows = jnp.repeat(cd.reshape(H // 2, 2), RET_QK_DIM, axis=1)
    cdm = cd_rows[:, :, None] * bmask[None]
    half = RET_QK_DIM // 2
    inv = ROPE_BASE ** (-jnp.arange(half, dtype=F32) / half)
    ang = jnp.arange(S).astype(F32)[:, None] * inv[None, :]
    cos, sin = jnp.cos(ang), jnp.sin(ang)
    cos_t = jnp.tile(cos, (1, LANES // half))
    sin_t = jnp.tile(jnp.concatenate([-sin, sin], axis=1), (1, LANES // RET_QK_DIM))
    return dmat, kdec_pair, qdec_pair, cdm, bmask, cos_t, sin_t


def _retention(proj3, C=RET_CHUNK):
    B, S, _ = proj3.shape
    dmat, kdec, qdec, cdm, bmask, cos_t, sin_t = _retention_tables(S, C)
    n_pairs = RET_HEADS // 2
    return pl.pallas_call(
        _retention_kernel,
        grid=(B, n_pairs, S // C),
        in_specs=[
            pl.BlockSpec((1, C, LANES), lambda b, p, c: (b, c, COL_RQ // LANES + p)),
            pl.BlockSpec((1, C, LANES), lambda b, p, c: (b, c, COL_RK // LANES + p)),
            pl.BlockSpec((1, C, 256), lambda b, p, c: (b, c, COL_RV // 256 + p)),
            pl.BlockSpec((1, C, 256), lambda b, p, c: (b, c, COL_RG // 256 + p)),
            pl.BlockSpec((C, LANES), lambda b, p, c: (c, 0)),
            pl.BlockSpec((C, LANES), lambda b, p, c: (c, 0)),
            pl.BlockSpec((2, C, C), lambda b, p, c: (p, 0, 0)),
            pl.BlockSpec((1, C, LANES), lambda b, p, c: (p, 0, 0)),
            pl.BlockSpec((1, C, 256), lambda b, p, c: (p, 0, 0)),
            pl.BlockSpec((1, LANES, 256), lambda b, p, c: (p, 0, 0)),
            pl.BlockSpec((LANES, 256), lambda b, p, c: (0, 0)),
        ],
        out_specs=pl.BlockSpec((1, C, 256), lambda b, p, c: (b, c, p)),
        out_shape=jax.ShapeDtypeStruct((B, S, RET_HEADS * RET_V_DIM), BF16),
        scratch_shapes=[pltpu.VMEM((LANES, 256), F32)],
        compiler_params=_cparams(("parallel", "parallel", "arbitrary")),
        name="retention",
    )(proj3, proj3, proj3, proj3, cos_t, sin_t, dmat, kdec, qdec, cdm, bmask)


def _head_rmsnorm(x, gain):
    R = x.shape[0]
    lane = lax.broadcasted_iota(jnp.int32, (R, LANES), 1)
    lo = lane < MOBA_HEAD_DIM
    outs = []
    for p in range(x.shape[1] // LANES):
        xs = x[:, p * LANES:(p + 1) * LANES]
        x2 = xs * xs
        m0 = jnp.sum(jnp.where(lo, x2, 0.0), axis=-1, keepdims=True) * (1.0 / MOBA_HEAD_DIM)
        m1 = jnp.sum(jnp.where(lo, 0.0, x2), axis=-1, keepdims=True) * (1.0 / MOBA_HEAD_DIM)
        r = jnp.where(lo, lax.rsqrt(m0 + NORM_EPS), lax.rsqrt(m1 + NORM_EPS))
        outs.append(xs * r * gain[:, p * LANES:(p + 1) * LANES])
    return jnp.concatenate(outs, axis=1)


def _moba_prep_kernel(q_ref, k_ref, v_ref, qg_ref, kg_ref, qn_ref, kn_ref, vb_ref, km_ref):
    qn = _head_rmsnorm(q_ref[0], qg_ref[...]) * (MOBA_HEAD_DIM ** -0.5)
    kn = _head_rmsnorm(k_ref[0], kg_ref[...])
    qn_ref[0] = qn.astype(BF16)
    kn_ref[0] = kn.astype(BF16)
    vb_ref[0] = v_ref[0].astype(BF16)
    km_ref[0, 0] = jnp.mean(kn, axis=0, keepdims=True)


def _moba_prep(proj3, qg, kg):
    B, S, _ = proj3.shape
    L = MOBA_BLOCK
    W = MOBA_HEADS * MOBA_HEAD_DIM
    nb = S // L
    blk = lambda col: pl.BlockSpec((1, L, W), lambda b, s: (b, s, col // W))
    out_blk = pl.BlockSpec((1, L, W), lambda b, s: (b, s, 0))
    return pl.pallas_call(
        _moba_prep_kernel,
        grid=(B, nb),
        in_specs=[blk(COL_MQ), blk(COL_MK), blk(COL_MV),
                  pl.BlockSpec((1, W), lambda b, s: (0, 0)),
                  pl.BlockSpec((1, W), lambda b, s: (0, 0))],
        out_specs=[out_blk, out_blk, out_blk,
                   pl.BlockSpec((1, 1, 1, W), lambda b, s: (b, s, 0, 0))],
        out_shape=[jax.ShapeDtypeStruct((B, S, W), BF16)] * 3
        + [jax.ShapeDtypeStruct((B, nb, 1, W), F32)],
        compiler_params=_cparams(("parallel", "parallel")),
        name="moba_prep",
    )(proj3, proj3, proj3, qg, kg)


def _moba_kernel(q_ref, k_ref, v_ref, km_ref, bias_ref, o_ref, mask_ref):
    L = MOBA_BLOCK
    qi = pl.program_id(2)
    nb = km_ref.shape[1]
    q = q_ref[0]
    lane = lax.broadcasted_iota(jnp.int32, (L, LANES), 1)
    km = km_ref[0].astype(BF16)
    km_pad = jnp.concatenate([km, jnp.zeros((LANES - nb, LANES), BF16)], axis=0)
    row = lax.broadcasted_iota(jnp.int32, (L, L), 0)
    col = lax.broadcasted_iota(jnp.int32, (L, L), 1)
    causal = row >= col
    blk_row = lax.broadcasted_iota(jnp.int32, (LANES, L), 0)
    outs = []
    for h in range(2):
        qh = jnp.where((lane // MOBA_HEAD_DIM) == h, q, jnp.zeros_like(q))
        g = lax.dot_general(qh, km_pad, (((1,), (1,)), ((), ())), preferred_element_type=F32)
        g = jnp.where(lane < qi, g, NEG_INF)
        cnt = jnp.zeros((L, LANES), F32)
        for m in range(nb):
            gm = g[:, m:m + 1]
            beats = jnp.where(gm > g, 1.0, jnp.where(gm == g, jnp.where(lane > m, 1.0, 0.0), 0.0))
            cnt = cnt + beats
        sel = jnp.where(cnt < MOBA_TOPK, jnp.where(lane < qi, 1.0, 0.0), 0.0).astype(BF16)
        for j in range(nb):
            onehot = jnp.where(blk_row == j, 1.0, 0.0).astype(BF16)
            mask_ref[j] = jnp.dot(sel, onehot, preferred_element_type=F32)

        def scores(j, bias_idx):
            kb = k_ref[0, pl.ds(pl.multiple_of(j * L, L), L), :]
            s = lax.dot_general(qh, kb, (((1,), (1,)), ((), ())), preferred_element_type=F32)
            return s + bias_ref[h, bias_idx]

        s = jnp.where(causal, scores(qi, 0), NEG_INF)
        m_run = jnp.max(s, axis=-1, keepdims=True)
        p = jnp.exp(s - m_run)
        l_run = jnp.sum(p, axis=-1, keepdims=True)
        vb = v_ref[0, pl.ds(pl.multiple_of(qi * L, L), L), :]
        acc = jnp.dot(p.astype(BF16), vb, preferred_element_type=F32)

        def body(j, carry):
            m_run, l_run, acc = carry
            s = scores(j, jnp.minimum(qi - j, N_BIAS_TILES - 1))
            s = jnp.where(mask_ref[j] > 0.5, s, NEG_INF)
            m_new = jnp.maximum(m_run, jnp.max(s, axis=-1, keepdims=True))
            alpha = jnp.exp(m_run - m_new)
            p = jnp.exp(s - m_new)
            l_new = alpha * l_run + jnp.sum(p, axis=-1, keepdims=True)
            vb = v_ref[0, pl.ds(pl.multiple_of(j * L, L), L), :]
            acc = alpha * acc + jnp.dot(p.astype(BF16), vb, preferred_element_type=F32)
            return m_new, l_new, acc

        m_run, l_run, acc = lax.fori_loop(0, qi, body, (m_run, l_run, acc))
        outs.append(acc / l_run)
    o_ref[0] = jnp.where(lane < MOBA_HEAD_DIM, outs[0], outs[1]).astype(BF16)


def _rel_bucket(dist):
    max_exact = REL_BUCKETS // 2
    n = jnp.maximum(dist, 0)
    nf = jnp.maximum(n, 1).astype(F32)
    large = max_exact + (jnp.log(nf / max_exact) / math.log(REL_MAX_DIST / max_exact)
                         * (REL_BUCKETS - max_exact)).astype(jnp.int32)
    large = jnp.minimum(large, REL_BUCKETS - 1)
    return jnp.where(n < max_exact, n, large)


def _bias_tiles(rel_bias):
    L = MOBA_BLOCK
    r = np.arange(L)[:, None] - np.arange(L)[None, :]
    dist = np.stack([d * L + r for d in range(N_BIAS_TILES)]).astype(np.int32)
    n_sat = (N_BIAS_TILES - 1) * L - (L - 1)
    sat = REL_BUCKETS // 2 + int(math.log(n_sat / (REL_BUCKETS // 2))
                                 / math.log(REL_MAX_DIST / (REL_BUCKETS // 2))
                                 * (REL_BUCKETS - REL_BUCKETS // 2) * (1 - 1e-6))
    assert sat >= REL_BUCKETS - 1
    bucket = _rel_bucket(jnp.asarray(dist))
    return jnp.take(rel_bias.T.astype(F32), bucket, axis=1)


def _moba(qn, kn, vb, kmean, bias_tiles):
    B, S, W = qn.shape
    L = MOBA_BLOCK
    nb = S // L
    n_pairs = MOBA_HEADS // 2
    return pl.pallas_call(
        _moba_kernel,
        grid=(B, n_pairs, nb),
        in_specs=[
            pl.BlockSpec((1, L, LANES), lambda b, p, i: (b, i, p)),
            pl.BlockSpec((1, S, LANES), lambda b, p, i: (b, 0, p)),
            pl.BlockSpec((1, S, LANES), lambda b, p, i: (b, 0, p)),
            pl.BlockSpec((1, nb, LANES), lambda b, p, i: (b, 0, p)),
            pl.BlockSpec((2, N_BIAS_TILES, L, L), lambda b, p, i: (p, 0, 0, 0)),
        ],
        out_specs=pl.BlockSpec((1, L, LANES), lambda b, p, i: (b, i, p)),
        out_shape=jax.ShapeDtypeStruct((B, S, W), BF16),
        scratch_shapes=[pltpu.VMEM((nb, L, L), F32)],
        compiler_params=_cparams(("parallel", "parallel", "arbitrary")),
        name="moba",
    )(qn, kn, vb, kmean, bias_tiles)


def _merge_kernel(x_ref, ya_ref, att_ref, ga_ref, gb_ref, wa_ref, wb_ref, wo_ref, g2_ref,
                  x1_ref, h2t_ref):
    ya = jnp.dot(ya_ref[...], wa_ref[...], preferred_element_type=F32)
    yb = jnp.dot(att_ref[...], wb_ref[...], preferred_element_type=F32)
    merged = jax.nn.sigmoid(ga_ref[...]) * ya + jax.nn.sigmoid(gb_ref[...]) * yb
    x1 = x_ref[...] + jnp.dot(merged.astype(BF16), wo_ref[...], preferred_element_type=F32)
    x1_ref[...] = x1
    ms = jnp.mean(x1 * x1, axis=-1, keepdims=True)
    h2 = x1 * lax.rsqrt(ms + NORM_EPS) * g2_ref[...]
    h2t_ref[...] = h2.T.astype(BF16)


def _merge(x2d, ya_in, att, proj, wa, wb, wo, g2, tm=256):
    T, D = x2d.shape
    full = lambda a: pl.BlockSpec(a.shape, lambda i: (0,) * a.ndim)
    return pl.pallas_call(
        _merge_kernel,
        grid=(T // tm,),
        in_specs=[
            pl.BlockSpec((tm, D), lambda i: (i, 0)),
            pl.BlockSpec((tm, ya_in.shape[1]), lambda i: (i, 0)),
            pl.BlockSpec((tm, att.shape[1]), lambda i: (i, 0)),
            pl.BlockSpec((tm, D), lambda i: (i, COL_GA // D)),
            pl.BlockSpec((tm, D), lambda i: (i, COL_GB // D)),
            full(wa), full(wb), full(wo), full(g2),
        ],
        out_specs=[pl.BlockSpec((tm, D), lambda i: (i, 0)),
                   pl.BlockSpec((D, tm), lambda i: (0, i))],
        out_shape=[jax.ShapeDtypeStruct((T, D), F32), jax.ShapeDtypeStruct((D, T), BF16)],
        compiler_params=_cparams(("parallel",)),
        name="merge",
    )(x2d, ya_in, att, proj, proj, wa, wb, wo, g2)


def _top_values(s, k):
    vals = []
    cur = s
    for i in range(k):
        m = jnp.max(cur, axis=0, keepdims=True)
        vals.append(m)
        if i + 1 < k:
            cur = jnp.where(cur == m, -BIG, cur)
    return vals


def _peer_prep_kernel(h2t_ref, wqt_ref, sk_ref, c_ref, s2_ref, e2_ref, w1_ref):
    K = PEER_TOPK
    nk = PEER_N_KEYS
    qt = jnp.dot(wqt_ref[...], h2t_ref[...], preferred_element_type=F32).astype(BF16)
    for h in range(PEER_HEADS):
        s1 = jnp.dot(sk_ref[2 * h], qt[(2 * h) * nk:(2 * h + 1) * nk], preferred_element_type=F32)
        s2 = jnp.dot(sk_ref[2 * h + 1], qt[(2 * h + 1) * nk:(2 * h + 2) * nk],
                     preferred_element_type=F32)
        v1 = _top_values(s1, K)
        v2 = _top_values(s2, K)
        V1 = jnp.concatenate(v1, axis=0)
        V2 = jnp.concatenate(v2, axis=0)
        groups = [v1[0] + V2, v1[1] + V2[0:8]]
        groups += [v1[i] + V2[0:8] for i in range(2, 8)]
        groups.append(V1[8:16] + v2[0])
        cand = jnp.concatenate(groups, axis=0)
        tops = _top_values(cand, K)
        tau = tops[K - 1]
        z = jnp.ones_like(tau)
        for t in tops[1:]:
            z = z + jnp.exp(t - tops[0])
        c = jnp.full(s1.shape, BIG, F32)
        for i in range(K):
            thr = jnp.min(jnp.where(v1[i] + V2 >= tau, V2, BIG), axis=0, keepdims=True)
            c = jnp.where(s1 == v1[i], thr, c)
        rows = slice(h * nk, (h + 1) * nk)
        c_ref[rows, :] = c
        s2_ref[rows, :] = s2
        e2_ref[rows, :] = jnp.exp(s2 - v2[0])
        w1_ref[rows, :] = jnp.exp(s1 - v1[0]) / z


def _peer_prep(h2t, wqt, sk, tp=256):
    D, T = h2t.shape
    rows = PEER_HEADS * PEER_N_KEYS
    full = lambda a: pl.BlockSpec(a.shape, lambda i: (0,) * a.ndim)
    out_blk = pl.BlockSpec((rows, tp), lambda i: (0, i))
    return pl.pallas_call(
        _peer_prep_kernel,
        grid=(T // tp,),
        in_specs=[pl.BlockSpec((D, tp), lambda i: (0, i)), full(wqt), full(sk)],
        out_specs=[out_blk] * 4,
        out_shape=[jax.ShapeDtypeStruct((rows, T), F32)] * 4,
        compiler_params=_cparams(("parallel",)),
        name="peer_prep",
    )(h2t, wqt, sk)


def _peer_kernel(h2t_ref, u_ref, vt_ref, c_ref, s2_ref, e2_ref, w1_ref, x1_ref, o_ref,
                 acc_ref, a_ref, ga_ref):
    e = pl.program_id(1)
    et, tl = a_ref.shape
    nk = PEER_N_KEYS
    rc = 64

    @pl.when(e == 0)
    def _():
        acc_ref[...] = jnp.zeros_like(acc_ref)

    a_ref[...] = jnp.dot(u_ref[...], h2t_ref[...], preferred_element_type=F32)
    n_ab = et // nk
    assert n_ab == 8
    for lc in range(tl // LANES):
        ls = slice(lc * LANES, (lc + 1) * LANES)
        a_rows = [pl.ds(pl.multiple_of(h * nk + e * n_ab, n_ab), n_ab)
                  for h in range(PEER_HEADS)]
        for ab in range(n_ab):
            crow = [c_ref[a_rows[h], ls][ab:ab + 1] for h in range(PEER_HEADS)]
            wrow = [w1_ref[a_rows[h], ls][ab:ab + 1] for h in range(PEER_HEADS)]
            for r in range(nk // rc):
                g = jnp.zeros((rc, LANES), F32)
                for h in range(PEER_HEADS):
                    rs = slice(h * nk + r * rc, h * nk + (r + 1) * rc)
                    g = g + jnp.where(s2_ref[rs, ls] >= crow[h], e2_ref[rs, ls] * wrow[h], 0.0)
                arows = slice(ab * nk + r * rc, ab * nk + (r + 1) * rc)
                x = a_ref[arows, ls]
                act = 0.5 * x * (1.0 + lax.erf(x * (1.0 / math.sqrt(2.0))))
                ga_ref[arows, ls] = (g * act).astype(BF16)
    acc_ref[...] += jnp.dot(vt_ref[...], ga_ref[...], preferred_element_type=F32)

    @pl.when(e == pl.num_programs(1) - 1)
    def _():
        o_ref[...] = x1_ref[...] + acc_ref[...].T


def _peer(h2t, u_b, vt_b, c, s2, e2, w1, x1, tl=512, et=1024):
    D, T = h2t.shape
    E = u_b.shape[0]
    rows = PEER_HEADS * PEER_N_KEYS
    tok = pl.BlockSpec((rows, tl), lambda i, e: (0, i))
    return pl.pallas_call(
        _peer_kernel,
        grid=(T // tl, E // et),
        in_specs=[
            pl.BlockSpec((D, tl), lambda i, e: (0, i)),
            pl.BlockSpec((et, D), lambda i, e: (e, 0)),
            pl.BlockSpec((D, et), lambda i, e: (0, e)),
            tok, tok, tok, tok,
            pl.BlockSpec((tl, D), lambda i, e: (i, 0)),
        ],
        out_specs=pl.BlockSpec((tl, D), lambda i, e: (i, 0)),
        out_shape=jax.ShapeDtypeStruct((T, D), F32),
        scratch_shapes=[pltpu.VMEM((D, tl), F32), pltpu.VMEM((et, tl), F32),
                        pltpu.VMEM((et, tl), BF16)],
        compiler_params=_cparams(("parallel", "arbitrary")),
        name="peer",
    )(h2t, u_b, vt_b, c, s2, e2, w1, x1)


def _layer(x, mix_g, w_in, ret_w, q_gain, k_gain, moba_w, rel_bias, w_out, ffn_g,
           peer_wq, peer_sk, peer_u, peer_v):
    B, S, D = x.shape
    T = B * S
    x2d = x.reshape(T, D)
    w_rot = jnp.concatenate([w_in[:, ORIG_GATE_START:], w_in[:, :ORIG_GATE_START]],
                            axis=1).astype(BF16)
    proj = _proj(x2d, mix_g.reshape(1, D), w_rot)
    proj3 = proj.reshape(B, S, IN_WIDTH)

    ya_in = _retention(proj3)

    qg = jnp.tile(q_gain, MOBA_HEADS).reshape(1, -1)
    kg = jnp.tile(k_gain, MOBA_HEADS).reshape(1, -1)
    qn, kn, vb, kmean = _moba_prep(proj3, qg, kg)
    att = _moba(qn, kn, vb, kmean.reshape(B, S // MOBA_BLOCK, -1), _bias_tiles(rel_bias))

    x1, h2t = _merge(x2d, ya_in.reshape(T, -1), att.reshape(T, -1), proj,
                     ret_w.astype(BF16), moba_w.astype(BF16), w_out.astype(BF16),
                     ffn_g.reshape(1, D))

    sk = peer_sk.reshape(2 * PEER_HEADS, PEER_N_KEYS, PEER_QUERY_DIM // 2).astype(BF16)
    c, s2, e2, w1 = _peer_prep(h2t, peer_wq.T.astype(BF16), sk)
    out = _peer(h2t, peer_u.astype(BF16), peer_v.T.astype(BF16), c, s2, e2, w1, x1)
    return out.reshape(B, S, D)


def kernel(x, mix_norm_g, w_in, ret_w_branch, moba_q_gain, moba_k_gain, moba_w_branch,
           rel_bias, w_out, ffn_norm_g, peer_w_q, peer_sub_keys, peer_u, peer_v):
    for l in range(mix_norm_g.shape[0]):
        x = _layer(x, mix_norm_g[l], w_in[l], ret_w_branch[l], moba_q_gain[l], moba_k_gain[l],
                   moba_w_branch[l], rel_bias, w_out[l], ffn_norm_g[l], peer_w_q[l],
                   peer_sub_keys[l], peer_u[l], peer_v[l])
    return x
```

```python
import functools
import math

import numpy as np
import jax
import jax.numpy as jnp
from jax import lax
from jax.experimental import pallas as pl
from jax.experimental.pallas import tpu as pltpu

F32 = jnp.float32
BF16 = jnp.bfloat16

D_MODEL = 1024
RET_HEADS = 8
RET_QK_DIM = 64
RET_V_DIM = 128
ROPE_BASE = 10000.0
MOBA_HEADS = 8
MOBA_HEAD_DIM = 64
MOBA_BLOCK = 256
MOBA_TOPK = 3
REL_BUCKETS = 32
REL_MAX_DIST = 2048
PEER_HEADS = 8
PEER_N_KEYS = 128
PEER_N_EXPERTS = PEER_N_KEYS * PEER_N_KEYS
PEER_QUERY_DIM = 256
PEER_TOPK = 16
NORM_EPS = 1e-6
NEG_INF = -1e30
BIG = 3.0e38

LANES = 128
VMEM_LIMIT = 56 * 1024 * 1024

COL_GA, COL_GB = 0, 1024
COL_RQ, COL_RK, COL_RV, COL_RG = 2048, 2560, 3072, 4096
COL_MQ, COL_MK, COL_MV = 5120, 5632, 6144
IN_WIDTH = 6656
ORIG_GATE_START = 4608

RET_CHUNK = 256
N_BIAS_TILES = 8


def _cparams(sem):
    return pltpu.CompilerParams(dimension_semantics=sem, vmem_limit_bytes=VMEM_LIMIT)


def _proj_kernel(x_ref, g_ref, w_ref, o_ref, h_ref):
    @pl.when(pl.program_id(1) == 0)
    def _():
        x = x_ref[...]
        ms = jnp.mean(x * x, axis=-1, keepdims=True)
        h_ref[...] = (x * lax.rsqrt(ms + NORM_EPS) * g_ref[...]).astype(BF16)

    o_ref[...] = jnp.dot(h_ref[...], w_ref[...], preferred_element_type=F32)


def _proj(x2d, g, w_bf16, tm=512, tn=1664):
    T, D = x2d.shape
    N = w_bf16.shape[1]
    return pl.pallas_call(
        _proj_kernel,
        grid=(T // tm, N // tn),
        in_specs=[
            pl.BlockSpec((tm, D), lambda i, j: (i, 0)),
            pl.BlockSpec((1, D), lambda i, j: (0, 0)),
            pl.BlockSpec((D, tn), lambda i, j: (0, j)),
        ],
        out_specs=pl.BlockSpec((tm, tn), lambda i, j: (i, j)),
        out_shape=jax.ShapeDtypeStruct((T, N), F32),
        scratch_shapes=[pltpu.VMEM((tm, D), BF16)],
        compiler_params=_cparams(("parallel", "arbitrary")),
        name="proj",
    )(x2d, g, w_bf16)


def _rope(x, cos, sin_signed, first_half):
    swapped = jnp.where(first_half, pltpu.roll(x, LANES - 32, 1), pltpu.roll(x, 32, 1))
    return x * cos + swapped * sin_signed


def _retention_kernel(q_ref, k_ref, v_ref, rg_ref, cos_ref, sin_ref, dmat_ref,
                      kdec_ref, qdec_ref, cdm_ref, bmask_ref, o_ref, state_ref):
    @pl.when(pl.program_id(2) == 0)
    def _():
        state_ref[...] = jnp.zeros_like(state_ref)

    C = q_ref.shape[1]
    lane = lax.broadcasted_iota(jnp.int32, (C, LANES), 1)
    first_half = (lane % RET_QK_DIM) < (RET_QK_DIM // 2)
    cos = cos_ref[...]
    sin = sin_ref[...]
    qr = _rope(q_ref[0], cos, sin, first_half)
    kr = _rope(k_ref[0], cos, sin, first_half) * (RET_QK_DIM ** -0.5)
    kr_b = kr.astype(BF16)
    v_b = v_ref[0].astype(BF16)
    state = state_ref[...]

    o_cross = jnp.dot(qr.astype(BF16), state.astype(BF16),
                      preferred_element_type=F32) * qdec_ref[0]
    rg = rg_ref[0]
    for h in range(2):
        qh = jnp.where((lane // RET_QK_DIM) == h, qr, 0.0).astype(BF16)
        s = lax.dot_general(qh, kr_b, (((1,), (1,)), ((), ())),
                            preferred_element_type=F32)
        s = s * dmat_ref[h]
        sl = slice(h * RET_V_DIM, (h + 1) * RET_V_DIM)
        o = jnp.dot(s.astype(BF16), v_b[:, sl], preferred_element_type=F32) + o_cross[:, sl]
        mu = jnp.mean(o, axis=-1, keepdims=True)
        d = o - mu
        var = jnp.mean(d * d, axis=-1, keepdims=True)
        y = d * lax.rsqrt(var + NORM_EPS)
        g = rg[:, sl]
        o_ref[0, :, sl] = (y * (g * jax.nn.sigmoid(g))).astype(BF16)

    kd = (kr * kdec_ref[0]).astype(BF16)
    kv = lax.dot_general(kd, v_b, (((0,), (0,)), ((), ())), preferred_element_type=F32)
    state_ref[...] = state * cdm_ref[0] + kv * bmask_ref[...]


def _retention_tables(S, C):
    H = RET_HEADS
    log_g = jnp.log1p(-jnp.exp2(-5.0 - jnp.arange(H, dtype=F32)))
    idx = jnp.arange(C, dtype=F32)
    rel = idx[:, None] - idx[None, :]
    dmat = jnp.where(rel[None] >= 0,
                     jnp.exp(jnp.maximum(rel, 0.0)[None] * log_g[:, None, None]), 0.0)
    kdec = jnp.exp((C - 1 - idx)[None, :] * log_g[:, None])
    qdec = jnp.exp((idx + 1.0)[None, :] * log_g[:, None])
    cd = jnp.exp(C * log_g)
    kdec_pair = jnp.repeat(kdec.reshape(H // 2, 2, C).transpose(0, 2, 1), RET_QK_DIM, axis=2)
    qdec_pair = jnp.repeat(qdec.reshape(H // 2, 2, C).transpose(0, 2, 1), RET_V_DIM, axis=2)
    rows = np.arange(2 * RET_QK_DIM)[:, None] // RET_QK_DIM
    cols = np.arange(2 * RET_V_DIM)[None, :] // RET_V_DIM
    bmask = jnp.asarray((rows == cols).astype(np.float32))
    cd_rows = jnp.repeat(cd.reshape(H // 2, 2), RET_QK_DIM, axis=1)
    cdm = cd_rows[:, :, None] * bmask[None]
    half = RET_QK_DIM // 2
    inv = ROPE_BASE ** (-jnp.arange(half, dtype=F32) / half)
    ang = jnp.arange(S).astype(F32)[:, None] * inv[None, :]
    cos, sin = jnp.cos(ang), jnp.sin(ang)
    cos_t = jnp.tile(cos, (1, LANES // half))
    sin_t = jnp.tile(jnp.concatenate([-sin, sin], axis=1), (1, LANES // RET_QK_DIM))
    return dmat, kdec_pair, qdec_pair, cdm, bmask, cos_t, sin_t


def _retention(proj3, C=RET_CHUNK):
    B, S, _ = proj3.shape
    dmat, kdec, qdec, cdm, bmask, cos_t, sin_t = _retention_tables(S, C)
    n_pairs = RET_HEADS // 2
    return pl.pallas_call(
        _retention_kernel,
        grid=(B, n_pairs, S // C),
        in_specs=[
            pl.BlockSpec((1, C, LANES), lambda b, p, c: (b, c, COL_RQ // LANES + p)),
            pl.BlockSpec((1, C, LANES), lambda b, p, c: (b, c, COL_RK // LANES + p)),
            pl.BlockSpec((1, C, 256), lambda b, p, c: (b, c, COL_RV // 256 + p)),
            pl.BlockSpec((1, C, 256), lambda b, p, c: (b, c, COL_RG // 256 + p)),
            pl.BlockSpec((C, LANES), lambda b, p, c: (c, 0)),
            pl.BlockSpec((C, LANES), lambda b, p, c: (c, 0)),
            pl.BlockSpec((2, C, C), lambda b, p, c: (p, 0, 0)),
            pl.BlockSpec((1, C, LANES), lambda b, p, c: (p, 0, 0)),
            pl.BlockSpec((1, C, 256), lambda b, p, c: (p, 0, 0)),
            pl.BlockSpec((1, LANES, 256), lambda b, p, c: (p, 0, 0)),
            pl.BlockSpec((LANES, 256), lambda b, p, c: (0, 0)),
        ],
        out_specs=pl.BlockSpec((1, C, 256), lambda b, p, c: (b, c, p)),
        out_shape=jax.ShapeDtypeStruct((B, S, RET_HEADS * RET_V_DIM), BF16),
        scratch_shapes=[pltpu.VMEM((LANES, 256), F32)],
        compiler_params=_cparams(("parallel", "parallel", "arbitrary")),
        name="retention",
    )(proj3, proj3, proj3, proj3, cos_t, sin_t, dmat, kdec, qdec, cdm, bmask)


def _head_rmsnorm(x, gain):
    R = x.shape[0]
    lane = lax.broadcasted_iota(jnp.int32, (R, LANES), 1)
    lo = lane < MOBA_HEAD_DIM
    outs = []
    for p in range(x.shape[1] // LANES):
        xs = x[:, p * LANES:(p + 1) * LANES]
        x2 = xs * xs
        m0 = jnp.sum(jnp.where(lo, x2, 0.0), axis=-1, keepdims=True) * (1.0 / MOBA_HEAD_DIM)
        m1 = jnp.sum(jnp.where(lo, 0.0, x2), axis=-1, keepdims=True) * (1.0 / MOBA_HEAD_DIM)
        r = jnp.where(lo, lax.rsqrt(m0 + NORM_EPS), lax.rsqrt(m1 + NORM_EPS))
        outs.append(xs * r * gain[:, p * LANES:(p + 1) * LANES])
    return jnp.concatenate(outs, axis=1)


def _moba_prep_kernel(q_ref, k_ref, v_ref, qg_ref, kg_ref, qn_ref, kn_ref, vt_ref, km_ref):
    qn = _head_rmsnorm(q_ref[0], qg_ref[...]) * (MOBA_HEAD_DIM ** -0.5)
    kn = _head_rmsnorm(k_ref[0], kg_ref[...])
    qn_ref[0] = qn.astype(BF16)
    kn_ref[0] = kn.astype(BF16)
    vt_ref[0, 0] = v_ref[0].T.astype(BF16)
    km_ref[0, 0] = jnp.mean(kn, axis=0, keepdims=True)


def _moba_prep(proj3, qg, kg):
    B, S, _ = proj3.shape
    L = MOBA_BLOCK
    W = MOBA_HEADS * MOBA_HEAD_DIM
    nb = S // L
    blk = lambda col: pl.BlockSpec((1, L, W), lambda b, s: (b, s, col // W))
    out_blk = pl.BlockSpec((1, L, W), lambda b, s: (b, s, 0))
    return pl.pallas_call(
        _moba_prep_kernel,
        grid=(B, nb),
        in_specs=[blk(COL_MQ), blk(COL_MK), blk(COL_MV),
                  pl.BlockSpec((1, W), lambda b, s: (0, 0)),
                  pl.BlockSpec((1, W), lambda b, s: (0, 0))],
        out_specs=[out_blk, out_blk,
                   pl.BlockSpec((1, 1, W, L), lambda b, s: (b, s, 0, 0)),
                   pl.BlockSpec((1, 1, 1, W), lambda b, s: (b, s, 0, 0))],
        out_shape=[jax.ShapeDtypeStruct((B, S, W), BF16)] * 2
        + [jax.ShapeDtypeStruct((B, nb, W, L), BF16),
           jax.ShapeDtypeStruct((B, nb, 1, W), F32)],
        compiler_params=_cparams(("parallel", "parallel")),
        name="moba_prep",
    )(proj3, proj3, proj3, qg, kg)


def _moba_kernel(q_ref, k_ref, vt_ref, km_ref, bias_ref, o_ref, sel_ref):
    L = MOBA_BLOCK
    qi = pl.program_id(2)
    nb = km_ref.shape[1]
    q = q_ref[0]
    lane = lax.broadcasted_iota(jnp.int32, (L, LANES), 1)
    km = km_ref[0].astype(BF16)
    blk = lax.broadcasted_iota(jnp.int32, (nb, L), 0)
    past = blk < qi
    key_row = lax.broadcasted_iota(jnp.int32, (L, L), 0)
    q_col = lax.broadcasted_iota(jnp.int32, (L, L), 1)
    causal = key_row <= q_col
    nt = (((1,), (1,)), ((), ()))
    qhs = []
    for h in range(2):
        qh = jnp.where((lane // MOBA_HEAD_DIM) == h, q, jnp.zeros_like(q))
        qhs.append(qh)
        g = lax.dot_general(km, qh, nt, preferred_element_type=F32)
        g = jnp.where(past, g, NEG_INF)
        cnt = jnp.zeros((nb, L), F32)
        for m in range(nb):
            gm = g[m:m + 1, :]
            beats = jnp.where(gm > g, 1.0, jnp.where(gm == g, jnp.where(blk > m, 1.0, 0.0), 0.0))
            cnt = cnt + beats
        sel = jnp.where(cnt < MOBA_TOPK, jnp.where(past, 1.0, 0.0), 0.0)
        for j in range(nb):
            sel_ref[h, j] = jnp.broadcast_to(sel[j:j + 1, :], (8, L))

    def scores(h, j, bias_idx):
        kb = k_ref[0, pl.ds(pl.multiple_of(j * L, L), L), :]
        s = lax.dot_general(kb, qhs[h], nt, preferred_element_type=F32)
        return s + bias_ref[h, bias_idx]

    carry = []
    for h in range(2):
        s = jnp.where(causal, scores(h, qi, 0), NEG_INF)
        m_run = jnp.max(s, axis=0, keepdims=True)
        p = jnp.exp(s - m_run)
        l_run = jnp.sum(p, axis=0, keepdims=True)
        acc = jnp.dot(vt_ref[0, qi], p.astype(BF16), preferred_element_type=F32)
        carry += [m_run, l_run, acc]

    def body(j, carry):
        out = []
        bias_idx = jnp.minimum(qi - j, N_BIAS_TILES - 1)
        vt = vt_ref[0, j]
        for h in range(2):
            m_run, l_run, acc = carry[3 * h:3 * h + 3]
            s = scores(h, j, bias_idx)
            s = jnp.where(sel_ref[h, j][0:1, :] > 0.5, s, NEG_INF)
            m_new = jnp.maximum(m_run, jnp.max(s, axis=0, keepdims=True))
            alpha = jnp.exp(m_run - m_new)
            p = jnp.exp(s - m_new)
            l_new = alpha * l_run + jnp.sum(p, axis=0, keepdims=True)
            acc = alpha * acc + jnp.dot(vt, p.astype(BF16), preferred_element_type=F32)
            out += [m_new, l_new, acc]
        return tuple(out)

    carry = lax.fori_loop(0, qi, body, tuple(carry))
    row = lax.broadcasted_iota(jnp.int32, (LANES, L), 0)
    ot = jnp.where(row < MOBA_HEAD_DIM, carry[2] / carry[1], carry[5] / carry[4])
    o_ref[0] = ot.T.astype(BF16)


def _rel_bucket(dist):
    max_exact = REL_BUCKETS // 2
    n = jnp.maximum(dist, 0)
    nf = jnp.maximum(n, 1).astype(F32)
    large = max_exact + (jnp.log(nf / max_exact) / math.log(REL_MAX_DIST / max_exact)
                         * (REL_BUCKETS - max_exact)).astype(jnp.int32)
    large = jnp.minimum(large, REL_BUCKETS - 1)
    return jnp.where(n < max_exact, n, large)


def _bias_kernel(rb_ref, bucket_ref, o_ref):
    h = pl.program_id(0)
    b = bucket_ref[...]
    acc = jnp.zeros(b.shape, F32)
    for k in range(REL_BUCKETS):
        acc = jnp.where(b == k, rb_ref[k, h], acc)
    o_ref[0] = acc


def _bias_tiles(rel_bias):
    L = MOBA_BLOCK
    r = np.arange(L)[None, :] - np.arange(L)[:, None]
    dist = np.stack([d * L + r for d in range(N_BIAS_TILES)]).astype(np.int32)
    n_sat = (N_BIAS_TILES - 1) * L - (L - 1)
    sat = REL_BUCKETS // 2 + int(math.log(n_sat / (REL_BUCKETS // 2))
                                 / math.log(REL_MAX_DIST / (REL_BUCKETS // 2))
                                 * (REL_BUCKETS - REL_BUCKETS // 2) * (1 - 1e-6))
    assert sat >= REL_BUCKETS - 1
    bucket = _rel_bucket(jnp.asarray(dist))
    return pl.pallas_call(
        _bias_kernel,
        grid=(MOBA_HEADS,),
        in_specs=[pl.BlockSpec(memory_space=pltpu.SMEM),
                  pl.BlockSpec(bucket.shape, lambda h: (0, 0, 0))],
        out_specs=pl.BlockSpec((1,) + bucket.shape, lambda h: (h, 0, 0, 0)),
        out_shape=jax.ShapeDtypeStruct((MOBA_HEADS,) + bucket.shape, F32),
        compiler_params=_cparams(("parallel",)),
        name="rel_bias_tiles",
    )(rel_bias.astype(F32), bucket)


def _moba(qn, kn, vt, kmean, bias_tiles):
    B, S, W = qn.shape
    L = MOBA_BLOCK
    nb = S // L
    n_pairs = MOBA_HEADS // 2
    return pl.pallas_call(
        _moba_kernel,
        grid=(B, n_pairs, nb),
        in_specs=[
            pl.BlockSpec((1, L, LANES), lambda b, p, i: (b, i, p)),
            pl.BlockSpec((1, S, LANES), lambda b, p, i: (b, 0, p)),
            pl.BlockSpec((1, nb, LANES, L), lambda b, p, i: (b, 0, p, 0)),
            pl.BlockSpec((1, nb, LANES), lambda b, p, i: (b, 0, p)),
            pl.BlockSpec((2, N_BIAS_TILES, L, L), lambda b, p, i: (p, 0, 0, 0)),
        ],
        out_specs=pl.BlockSpec((1, L, LANES), lambda b, p, i: (b, i, p)),
        out_shape=jax.ShapeDtypeStruct((B, S, W), BF16),
        scratch_shapes=[pltpu.VMEM((2, nb, 8, L), F32)],
        compiler_params=_cparams(("parallel", "parallel", "arbitrary")),
        name="moba",
    )(qn, kn, vt, kmean, bias_tiles)


def _merge_kernel(x_ref, ya_ref, att_ref, ga_ref, gb_ref, wa_ref, wb_ref, wo_ref, g2_ref,
                  x1_ref, h2t_ref):
    ya = jnp.dot(ya_ref[...], wa_ref[...], preferred_element_type=F32)
    yb = jnp.dot(att_ref[...], wb_ref[...], preferred_element_type=F32)
    merged = jax.nn.sigmoid(ga_ref[...]) * ya + jax.nn.sigmoid(gb_ref[...]) * yb
    x1 = x_ref[...] + jnp.dot(merged.astype(BF16), wo_ref[...], preferred_element_type=F32)
    x1_ref[...] = x1
    ms = jnp.mean(x1 * x1, axis=-1, keepdims=True)
    h2 = x1 * lax.rsqrt(ms + NORM_EPS) * g2_ref[...]
    h2t_ref[...] = h2.T.astype(BF16)


def _merge(x2d, ya_in, att, proj, wa, wb, wo, g2, tm=256):
    T, D = x2d.shape
    full = lambda a: pl.BlockSpec(a.shape, lambda i: (0,) * a.ndim)
    return pl.pallas_call(
        _merge_kernel,
        grid=(T // tm,),
        in_specs=[
            pl.BlockSpec((tm, D), lambda i: (i, 0)),
            pl.BlockSpec((tm, ya_in.shape[1]), lambda i: (i, 0)),
            pl.BlockSpec((tm, att.shape[1]), lambda i: (i, 0)),
            pl.BlockSpec((tm, D), lambda i: (i, COL_GA // D)),
            pl.BlockSpec((tm, D), lambda i: (i, COL_GB // D)),
            full(wa), full(wb), full(wo), full(g2),
        ],
        out_specs=[pl.BlockSpec((tm, D), lambda i: (i, 0)),
                   pl.BlockSpec((D, tm), lambda i: (0, i))],
        out_shape=[jax.ShapeDtypeStruct((T, D), F32), jax.ShapeDtypeStruct((D, T), BF16)],
        compiler_params=_cparams(("parallel",)),
        name="merge",
    )(x2d, ya_in, att, proj, proj, wa, wb, wo, g2)


def _top_values(s, k):
    vals = []
    cur = s
    for i in range(k):
        m = jnp.max(cur, axis=0, keepdims=True)
        vals.append(m)
        if i + 1 < k:
            cur = jnp.where(cur == m, -BIG, cur)
    return vals


def _peer_prep_kernel(h2t_ref, wqt_ref, sk_ref, c_ref, s2_ref, e2_ref, w1_ref):
    K = PEER_TOPK
    nk = PEER_N_KEYS
    qt = jnp.dot(wqt_ref[...], h2t_ref[...], preferred_element_type=F32).astype(BF16)
    for h in range(PEER_HEADS):
        s1 = jnp.dot(sk_ref[2 * h], qt[(2 * h) * nk:(2 * h + 1) * nk], preferred_element_type=F32)
        s2 = jnp.dot(sk_ref[2 * h + 1], qt[(2 * h + 1) * nk:(2 * h + 2) * nk],
                     preferred_element_type=F32)
        v1 = _top_values(s1, K)
        v2 = _top_values(s2, K)
        V1 = jnp.concatenate(v1, axis=0)
        V2 = jnp.concatenate(v2, axis=0)
        groups = [v1[0] + V2, v1[1] + V2[0:8]]
        groups += [v1[i] + V2[0:8] for i in range(2, 8)]
        groups.append(V1[8:16] + v2[0])
        cand = jnp.concatenate(groups, axis=0)
        tops = _top_values(cand, K)
        tau = tops[K - 1]
        z = jnp.ones_like(tau)
        for t in tops[1:]:
            z = z + jnp.exp(t - tops[0])
        c = jnp.full(s1.shape, BIG, F32)
        for i in range(K):
            thr = jnp.min(jnp.where(v1[i] + V2 >= tau, V2, BIG), axis=0, keepdims=True)
            c = jnp.where(s1 == v1[i], thr, c)
        rows = slice(h * nk, (h + 1) * nk)
        c_ref[rows, :] = c
        s2_ref[rows, :] = s2
        e2_ref[rows, :] = jnp.exp(s2 - v2[0])
        w1_ref[rows, :] = jnp.exp(s1 - v1[0]) / z


def _peer_prep(h2t, wqt, sk, tp=256):
    D, T = h2t.shape
    rows = PEER_HEADS * PEER_N_KEYS
    full = lambda a: pl.BlockSpec(a.shape, lambda i: (0,) * a.ndim)
    out_blk = pl.BlockSpec((rows, tp), lambda i: (0, i))
    return pl.pallas_call(
        _peer_prep_kernel,
        grid=(T // tp,),
        in_specs=[pl.BlockSpec((D, tp), lambda i: (0, i)), full(wqt), full(sk)],
        out_specs=[out_blk] * 4,
        out_shape=[jax.ShapeDtypeStruct((rows, T), F32)] * 4,
        compiler_params=_cparams(("parallel",)),
        name="peer_prep",
    )(h2t, wqt, sk)


def _peer_kernel(h2t_ref, u_ref, vt_ref, c_ref, s2_ref, e2_ref, w1_ref, x1_ref, o_ref,
                 acc_ref, a_ref, ga_ref):
    e = pl.program_id(1)
    et, tl = a_ref.shape
    nk = PEER_N_KEYS
    rc = 64

    @pl.when(e == 0)
    def _():
        acc_ref[...] = jnp.zeros_like(acc_ref)

    a_ref[...] = jnp.dot(u_ref[...], h2t_ref[...], preferred_element_type=F32)
    n_ab = et // nk
    assert n_ab == 8
    for lc in range(tl // LANES):
        ls = slice(lc * LANES, (lc + 1) * LANES)
        a_rows = [pl.ds(pl.multiple_of(h * nk + e * n_ab, n_ab), n_ab)
                  for h in range(PEER_HEADS)]
        for ab in range(n_ab):
            crow = [c_ref[a_rows[h], ls][ab:ab + 1] for h in range(PEER_HEADS)]
            wrow = [w1_ref[a_rows[h], ls][ab:ab + 1] for h in range(PEER_HEADS)]
            for r in range(nk // rc):
                g = jnp.zeros((rc, LANES), F32)
                for h in range(PEER_HEADS):
                    rs = slice(h * nk + r * rc, h * nk + (r + 1) * rc)
                    g = g + jnp.where(s2_ref[rs, ls] >= crow[h], e2_ref[rs, ls] * wrow[h], 0.0)
                arows = slice(ab * nk + r * rc, ab * nk + (r + 1) * rc)
                x = a_ref[arows, ls]
                act = 0.5 * x * (1.0 + lax.erf(x * (1.0 / math.sqrt(2.0))))
                ga_ref[arows, ls] = (g * act).astype(BF16)
    acc_ref[...] += jnp.dot(vt_ref[...], ga_ref[...], preferred_element_type=F32)

    @pl.when(e == pl.num_programs(1) - 1)
    def _():
        o_ref[...] = x1_ref[...] + acc_ref[...].T


def _peer(h2t, u_b, vt_b, c, s2, e2, w1, x1, tl=512, et=1024):
    D, T = h2t.shape
    E = u_b.shape[0]
    rows = PEER_HEADS * PEER_N_KEYS
    tok = pl.BlockSpec((rows, tl), lambda i, e: (0, i))
    return pl.pallas_call(
        _peer_kernel,
        grid=(T // tl, E // et),
        in_specs=[
            pl.BlockSpec((D, tl), lambda i, e: (0, i)),
            pl.BlockSpec((et, D), lambda i, e: (e, 0)),
            pl.BlockSpec((D, et), lambda i, e: (0, e)),
            tok, tok, tok, tok,
            pl.BlockSpec((tl, D), lambda i, e: (i, 0)),
        ],
        out_specs=pl.BlockSpec((tl, D), lambda i, e: (i, 0)),
        out_shape=jax.ShapeDtypeStruct((T, D), F32),
        scratch_shapes=[pltpu.VMEM((D, tl), F32), pltpu.VMEM((et, tl), F32),
                        pltpu.VMEM((et, tl), BF16)],
        compiler_params=_cparams(("parallel", "arbitrary")),
        name="peer",
    )(h2t, u_b, vt_b, c, s2, e2, w1, x1)


def _layer(x, mix_g, w_in, ret_w, q_gain, k_gain, moba_w, rel_bias, w_out, ffn_g,
           peer_wq, peer_sk, peer_u, peer_v):
    B, S, D = x.shape
    T = B * S
    x2d = x.reshape(T, D)
    w_rot = jnp.concatenate([w_in[:, ORIG_GATE_START:], w_in[:, :ORIG_GATE_START]],
                            axis=1).astype(BF16)
    proj = _proj(x2d, mix_g.reshape(1, D), w_rot)
    proj3 = proj.reshape(B, S, IN_WIDTH)

    ya_in = _retention(proj3)

    qg = jnp.tile(q_gain, MOBA_HEADS).reshape(1, -1)
    kg = jnp.tile(k_gain, MOBA_HEADS).reshape(1, -1)
    qn, kn, vt, kmean = _moba_prep(proj3, qg, kg)
    att = _moba(qn, kn, vt, kmean.reshape(B, S // MOBA_BLOCK, -1), _bias_tiles(rel_bias))

    x1, h2t = _merge(x2d, ya_in.reshape(T, -1), att.reshape(T, -1), proj,
                     ret_w.astype(BF16), moba_w.astype(BF16), w_out.astype(BF16),
                     ffn_g.reshape(1, D))

    sk = peer_sk.reshape(2 * PEER_HEADS, PEER_N_KEYS, PEER_QUERY_DIM // 2).astype(BF16)
    c, s2, e2, w1 = _peer_prep(h2t, peer_wq.T.astype(BF16), sk)
    out = _peer(h2t, peer_u.astype(BF16), peer_v.T.astype(BF16), c, s2, e2, w1, x1)
    return out.reshape(B, S, D)


def kernel(x, mix_norm_g, w_in, ret_w_branch, moba_q_gain, moba_k_gain, moba_w_branch,
           rel_bias, w_out, ffn_norm_g, peer_w_q, peer_sub_keys, peer_u, peer_v):
    for l in range(mix_norm_g.shape[0]):
        x = _layer(x, mix_norm_g[l], w_in[l], ret_w_branch[l], moba_q_gain[l], moba_k_gain[l],
                   moba_w_branch[l], rel_bias, w_out[l], ffn_norm_g[l], peer_w_q[l],
                   peer_sub_keys[l], peer_u[l], peer_v[l])
    return x
```

```python
import functools
import math

import numpy as np
import jax
import jax.numpy as jnp
from jax import lax
from jax.experimental import pallas as pl
from jax.experimental.pallas import tpu as pltpu

F32 = jnp.float32
BF16 = jnp.bfloat16

D_MODEL = 1024
RET_HEADS = 8
RET_QK_DIM = 64
RET_V_DIM = 128
ROPE_BASE = 10000.0
MOBA_HEADS = 8
MOBA_HEAD_DIM = 64
MOBA_BLOCK = 256
MOBA_TOPK = 3
REL_BUCKETS = 32
REL_MAX_DIST = 2048
PEER_HEADS = 8
PEER_N_KEYS = 128
PEER_N_EXPERTS = PEER_N_KEYS * PEER_N_KEYS
PEER_QUERY_DIM = 256
PEER_TOPK = 16
NORM_EPS = 1e-6
NEG_INF = -1e30
BIG = 3.0e38

LANES = 128
VMEM_LIMIT = 56 * 1024 * 1024

COL_GA, COL_GB = 0, 1024
COL_RQ, COL_RK, COL_RV, COL_RG = 2048, 2560, 3072, 4096
COL_MQ, COL_MK, COL_MV = 5120, 5632, 6144
IN_WIDTH = 6656
ORIG_GATE_START = 4608

RET_CHUNK = 256
N_BIAS_TILES = 8


def _cparams(sem):
    return pltpu.CompilerParams(dimension_semantics=sem, vmem_limit_bytes=VMEM_LIMIT)


def _proj_kernel(x_ref, g_ref, w_ref, o_ref, h_ref):
    @pl.when(pl.program_id(1) == 0)
    def _():
        x = x_ref[...]
        ms = jnp.mean(x * x, axis=-1, keepdims=True)
        h_ref[...] = (x * lax.rsqrt(ms + NORM_EPS) * g_ref[...]).astype(BF16)

    o_ref[...] = jnp.dot(h_ref[...], w_ref[...], preferred_element_type=F32)


def _proj(x2d, g, w_bf16, tm=512, tn=1664):
    T, D = x2d.shape
    N = w_bf16.shape[1]
    return pl.pallas_call(
        _proj_kernel,
        grid=(T // tm, N // tn),
        in_specs=[
            pl.BlockSpec((tm, D), lambda i, j: (i, 0)),
            pl.BlockSpec((1, D), lambda i, j: (0, 0)),
            pl.BlockSpec((D, tn), lambda i, j: (0, j)),
        ],
        out_specs=pl.BlockSpec((tm, tn), lambda i, j: (i, j)),
        out_shape=jax.ShapeDtypeStruct((T, N), F32),
        scratch_shapes=[pltpu.VMEM((tm, D), BF16)],
        compiler_params=_cparams(("parallel", "arbitrary")),
        name="proj",
    )(x2d, g, w_bf16)


def _rope(x, cos, sin_signed, first_half):
    swapped = jnp.where(first_half, pltpu.roll(x, LANES - 32, 1), pltpu.roll(x, 32, 1))
    return x * cos + swapped * sin_signed


def _retention_kernel(q_ref, k_ref, v_ref, rg_ref, cos_ref, sin_ref, dmat_ref,
                      kdec_ref, qdec_ref, cdm_ref, bmask_ref, o_ref, state_ref):
    @pl.when(pl.program_id(2) == 0)
    def _():
        state_ref[...] = jnp.zeros_like(state_ref)

    C = q_ref.shape[1]
    lane = lax.broadcasted_iota(jnp.int32, (C, LANES), 1)
    first_half = (lane % RET_QK_DIM) < (RET_QK_DIM // 2)
    cos = cos_ref[...]
    sin = sin_ref[...]
    qr = _rope(q_ref[0], cos, sin, first_half)
    kr = _rope(k_ref[0], cos, sin, first_half) * (RET_QK_DIM ** -0.5)
    kr_b = kr.astype(BF16)
    v_b = v_ref[0].astype(BF16)
    state = state_ref[...]

    o_cross = jnp.dot(qr.astype(BF16), state.astype(BF16),
                      preferred_element_type=F32) * qdec_ref[0]
    rg = rg_ref[0]
    for h in range(2):
        qh = jnp.where((lane // RET_QK_DIM) == h, qr, 0.0).astype(BF16)
        s = lax.dot_general(qh, kr_b, (((1,), (1,)), ((), ())),
                            preferred_element_type=F32)
        s = s * dmat_ref[h]
        sl = slice(h * RET_V_DIM, (h + 1) * RET_V_DIM)
        o = jnp.dot(s.astype(BF16), v_b[:, sl], preferred_element_type=F32) + o_cross[:, sl]
        mu = jnp.mean(o, axis=-1, keepdims=True)
        d = o - mu
        var = jnp.mean(d * d, axis=-1, keepdims=True)
        y = d * lax.rsqrt(var + NORM_EPS)
        g = rg[:, sl]
        o_ref[0, :, sl] = (y * (g * jax.nn.sigmoid(g))).astype(BF16)

    kd = (kr * kdec_ref[0]).astype(BF16)
    kv = lax.dot_general(kd, v_b, (((0,), (0,)), ((), ())), preferred_element_type=F32)
    state_ref[...] = state * cdm_ref[0] + kv * bmask_ref[...]


def _retention_tables(S, C):
    H = RET_HEADS
    log_g = jnp.log1p(-jnp.exp2(-5.0 - jnp.arange(H, dtype=F32)))
    idx = jnp.arange(C, dtype=F32)
    rel = idx[:, None] - idx[None, :]
    dmat = jnp.where(rel[None] >= 0,
                     jnp.exp(jnp.maximum(rel, 0.0)[None] * log_g[:, None, None]), 0.0)
    kdec = jnp.exp((C - 1 - idx)[None, :] * log_g[:, None])
    qdec = jnp.exp((idx + 1.0)[None, :] * log_g[:, None])
    cd = jnp.exp(C * log_g)
    kdec_pair = jnp.repeat(kdec.reshape(H // 2, 2, C).transpose(0, 2, 1), RET_QK_DIM, axis=2)
    qdec_pair = jnp.repeat(qdec.reshape(H // 2, 2, C).transpose(0, 2, 1), RET_V_DIM, axis=2)
    rows = np.arange(2 * RET_QK_DIM)[:, None] // RET_QK_DIM
    cols = np.arange(2 * RET_V_DIM)[None, :] // RET_V_DIM
    bmask = jnp.asarray((rows == cols).astype(np.float32))
    cd_rows = jnp.repeat(cd.reshape(H // 2, 2), RET_QK_DIM, axis=1)
    cdm = cd_rows[:, :, None] * bmask[None]
    half = RET_QK_DIM // 2
    inv = ROPE_BASE ** (-jnp.arange(half, dtype=F32) / half)
    ang = jnp.arange(S).astype(F32)[:, None] * inv[None, :]
    cos, sin = jnp.cos(ang), jnp.sin(ang)
    cos_t = jnp.tile(cos, (1, LANES // half))
    sin_t = jnp.tile(jnp.concatenate([-sin, sin], axis=1), (1, LANES // RET_QK_DIM))
    return dmat, kdec_pair, qdec_pair, cdm, bmask, cos_t, sin_t


def _retention(proj3, C=RET_CHUNK):
    B, S, _ = proj3.shape
    dmat, kdec, qdec, cdm, bmask, cos_t, sin_t = _retention_tables(S, C)
    n_pairs = RET_HEADS // 2
    return pl.pallas_call(
        _retention_kernel,
        grid=(B, n_pairs, S // C),
        in_specs=[
            pl.BlockSpec((1, C, LANES), lambda b, p, c: (b, c, COL_RQ // LANES + p)),
            pl.BlockSpec((1, C, LANES), lambda b, p, c: (b, c, COL_RK // LANES + p)),
            pl.BlockSpec((1, C, 256), lambda b, p, c: (b, c, COL_RV // 256 + p)),
            pl.BlockSpec((1, C, 256), lambda b, p, c: (b, c, COL_RG // 256 + p)),
            pl.BlockSpec((C, LANES), lambda b, p, c: (c, 0)),
            pl.BlockSpec((C, LANES), lambda b, p, c: (c, 0)),
            pl.BlockSpec((2, C, C), lambda b, p, c: (p, 0, 0)),
            pl.BlockSpec((1, C, LANES), lambda b, p, c: (p, 0, 0)),
            pl.BlockSpec((1, C, 256), lambda b, p, c: (p, 0, 0)),
            pl.BlockSpec((1, LANES, 256), lambda b, p, c: (p, 0, 0)),
            pl.BlockSpec((LANES, 256), lambda b, p, c: (0, 0)),
        ],
        out_specs=pl.BlockSpec((1, C, 256), lambda b, p, c: (b, c, p)),
        out_shape=jax.ShapeDtypeStruct((B, S, RET_HEADS * RET_V_DIM), BF16),
        scratch_shapes=[pltpu.VMEM((LANES, 256), F32)],
        compiler_params=_cparams(("parallel", "parallel", "arbitrary")),
        name="retention",
    )(proj3, proj3, proj3, proj3, cos_t, sin_t, dmat, kdec, qdec, cdm, bmask)


def _head_rmsnorm(x, gain):
    R = x.shape[0]
    lane = lax.broadcasted_iota(jnp.int32, (R, LANES), 1)
    lo = lane < MOBA_HEAD_DIM
    outs = []
    for p in range(x.shape[1] // LANES):
        xs = x[:, p * LANES:(p + 1) * LANES]
        x2 = xs * xs
        m0 = jnp.sum(jnp.where(lo, x2, 0.0), axis=-1, keepdims=True) * (1.0 / MOBA_HEAD_DIM)
        m1 = jnp.sum(jnp.where(lo, 0.0, x2), axis=-1, keepdims=True) * (1.0 / MOBA_HEAD_DIM)
        r = jnp.where(lo, lax.rsqrt(m0 + NORM_EPS), lax.rsqrt(m1 + NORM_EPS))
        outs.append(xs * r * gain[:, p * LANES:(p + 1) * LANES])
    return jnp.concatenate(outs, axis=1)


def _moba_prep_kernel(q_ref, k_ref, v_ref, qg_ref, kg_ref, qn_ref, kn_ref, vt_ref, km_ref):
    qn = _head_rmsnorm(q_ref[0], qg_ref[...]) * (MOBA_HEAD_DIM ** -0.5)
    kn = _head_rmsnorm(k_ref[0], kg_ref[...])
    qn_ref[0] = qn.astype(BF16)
    kn_ref[0] = kn.astype(BF16)
    vt_ref[0, 0] = v_ref[0].T.astype(BF16)
    km_ref[0, 0] = jnp.mean(kn, axis=0, keepdims=True)


def _moba_prep(proj3, qg, kg):
    B, S, _ = proj3.shape
    L = MOBA_BLOCK
    W = MOBA_HEADS * MOBA_HEAD_DIM
    nb = S // L
    blk = lambda col: pl.BlockSpec((1, L, W), lambda b, s: (b, s, col // W))
    out_blk = pl.BlockSpec((1, L, W), lambda b, s: (b, s, 0))
    return pl.pallas_call(
        _moba_prep_kernel,
        grid=(B, nb),
        in_specs=[blk(COL_MQ), blk(COL_MK), blk(COL_MV),
                  pl.BlockSpec((1, W), lambda b, s: (0, 0)),
                  pl.BlockSpec((1, W), lambda b, s: (0, 0))],
        out_specs=[out_blk, out_blk,
                   pl.BlockSpec((1, 1, W, L), lambda b, s: (b, s, 0, 0)),
                   pl.BlockSpec((1, 1, 1, W), lambda b, s: (b, s, 0, 0))],
        out_shape=[jax.ShapeDtypeStruct((B, S, W), BF16)] * 2
        + [jax.ShapeDtypeStruct((B, nb, W, L), BF16),
           jax.ShapeDtypeStruct((B, nb, 1, W), F32)],
        compiler_params=_cparams(("parallel", "parallel")),
        name="moba_prep",
    )(proj3, proj3, proj3, qg, kg)


def _moba_kernel(q_ref, k_ref, vt_ref, km_ref, bias_ref, o_ref, sel_ref, s_ref, acc_ref):
    L = MOBA_BLOCK
    qi = pl.program_id(2)
    nb = km_ref.shape[1]
    q = q_ref[0]
    lane = lax.broadcasted_iota(jnp.int32, (L, LANES), 1)
    km = km_ref[0].astype(BF16)
    blk = lax.broadcasted_iota(jnp.int32, (nb, L), 0)
    past = blk < qi
    key_row = lax.broadcasted_iota(jnp.int32, (L, L), 0)
    q_col = lax.broadcasted_iota(jnp.int32, (L, L), 1)
    causal = key_row <= q_col
    nt = (((1,), (1,)), ((), ()))
    qhs = []
    for h in range(2):
        qh = jnp.where((lane // MOBA_HEAD_DIM) == h, q, jnp.zeros_like(q))
        qhs.append(qh)
        g = lax.dot_general(km, qh, nt, preferred_element_type=F32)
        g = jnp.where(past, g, NEG_INF)
        cnt = jnp.zeros((nb, L), F32)
        for m in range(nb):
            gm = g[m:m + 1, :]
            beats = jnp.where(gm > g, 1.0, jnp.where(gm == g, jnp.where(blk > m, 1.0, 0.0), 0.0))
            cnt = cnt + beats
        sel = jnp.where(cnt < MOBA_TOPK, jnp.where(past, 1.0, 0.0), 0.0)
        for j in range(nb):
            sel_ref[h, j] = jnp.broadcast_to(sel[j:j + 1, :], (8, L))

    def scores(h, j, bias_idx):
        kb = k_ref[0, pl.ds(pl.multiple_of(j * L, L), L), :]
        s = lax.dot_general(kb, qhs[h], nt, preferred_element_type=F32)
        return s + bias_ref[h, bias_idx]

    def fold(x, op):
        return op(x.reshape(L // 8, 8, L), axis=0)

    mx = []
    for h in range(2):
        s = jnp.where(causal, scores(h, qi, 0), NEG_INF)
        s_ref[h, qi] = s
        mx.append(fold(s, jnp.max))

    def pass1(j, mx):
        bias_idx = jnp.minimum(qi - j, N_BIAS_TILES - 1)
        out = []
        for h in range(2):
            s = jnp.where(sel_ref[h, j][0:1, :] > 0.5, scores(h, j, bias_idx), NEG_INF)
            s_ref[h, j] = s
            out.append(jnp.maximum(mx[h], fold(s, jnp.max)))
        return tuple(out)

    mx = lax.fori_loop(0, qi, pass1, tuple(mx))
    m = [jnp.max(x, axis=0, keepdims=True) for x in mx]

    acc_ref[...] = jnp.zeros_like(acc_ref)

    def pass2(j, ls):
        vt = vt_ref[0, j]
        out = []
        for h in range(2):
            p = jnp.exp(s_ref[h, j] - m[h])
            out.append(ls[h] + fold(p, jnp.sum))
            acc_ref[h] += jnp.dot(vt, p.astype(BF16), preferred_element_type=F32)
        return tuple(out)

    zeros = jnp.zeros((8, L), F32)
    ls = lax.fori_loop(0, qi + 1, pass2, (zeros, zeros))
    l = [jnp.sum(x, axis=0, keepdims=True) for x in ls]
    row = lax.broadcasted_iota(jnp.int32, (LANES, L), 0)
    ot = jnp.where(row < MOBA_HEAD_DIM, acc_ref[0] / l[0], acc_ref[1] / l[1])
    o_ref[0] = ot.T.astype(BF16)


def _rel_bucket(dist):
    max_exact = REL_BUCKETS // 2
    n = jnp.maximum(dist, 0)
    nf = jnp.maximum(n, 1).astype(F32)
    large = max_exact + (jnp.log(nf / max_exact) / math.log(REL_MAX_DIST / max_exact)
                         * (REL_BUCKETS - max_exact)).astype(jnp.int32)
    large = jnp.minimum(large, REL_BUCKETS - 1)
    return jnp.where(n < max_exact, n, large)


def _bias_kernel(rb_ref, bucket_ref, o_ref):
    h = pl.program_id(0)
    b = bucket_ref[...]
    acc = jnp.zeros(b.shape, F32)
    for k in range(REL_BUCKETS):
        acc = jnp.where(b == k, rb_ref[k, h], acc)
    o_ref[0] = acc


def _bias_tiles(rel_bias):
    L = MOBA_BLOCK
    r = np.arange(L)[None, :] - np.arange(L)[:, None]
    dist = np.stack([d * L + r for d in range(N_BIAS_TILES)]).astype(np.int32)
    n_sat = (N_BIAS_TILES - 1) * L - (L - 1)
    sat = REL_BUCKETS // 2 + int(math.log(n_sat / (REL_BUCKETS // 2))
                                 / math.log(REL_MAX_DIST / (REL_BUCKETS // 2))
                                 * (REL_BUCKETS - REL_BUCKETS // 2) * (1 - 1e-6))
    assert sat >= REL_BUCKETS - 1
    bucket = _rel_bucket(jnp.asarray(dist))
    return pl.pallas_call(
        _bias_kernel,
        grid=(MOBA_HEADS,),
        in_specs=[pl.BlockSpec(memory_space=pltpu.SMEM),
                  pl.BlockSpec(bucket.shape, lambda h: (0, 0, 0))],
        out_specs=pl.BlockSpec((1,) + bucket.shape, lambda h: (h, 0, 0, 0)),
        out_shape=jax.ShapeDtypeStruct((MOBA_HEADS,) + bucket.shape, F32),
        compiler_params=_cparams(("parallel",)),
        name="rel_bias_tiles",
    )(rel_bias.astype(F32), bucket)


def _moba(qn, kn, vt, kmean, bias_tiles):
    B, S, W = qn.shape
    L = MOBA_BLOCK
    nb = S // L
    n_pairs = MOBA_HEADS // 2
    return pl.pallas_call(
        _moba_kernel,
        grid=(B, n_pairs, nb),
        in_specs=[
            pl.BlockSpec((1, L, LANES), lambda b, p, i: (b, i, p)),
            pl.BlockSpec((1, S, LANES), lambda b, p, i: (b, 0, p)),
            pl.BlockSpec((1, nb, LANES, L), lambda b, p, i: (b, 0, p, 0)),
            pl.BlockSpec((1, nb, LANES), lambda b, p, i: (b, 0, p)),
            pl.BlockSpec((2, N_BIAS_TILES, L, L), lambda b, p, i: (p, 0, 0, 0)),
        ],
        out_specs=pl.BlockSpec((1, L, LANES), lambda b, p, i: (b, i, p)),
        out_shape=jax.ShapeDtypeStruct((B, S, W), BF16),
        scratch_shapes=[pltpu.VMEM((2, nb, 8, L), F32), pltpu.VMEM((2, nb, L, L), F32),
                        pltpu.VMEM((2, LANES, L), F32)],
        compiler_params=_cparams(("parallel", "parallel", "arbitrary")),
        name="moba",
    )(qn, kn, vt, kmean, bias_tiles)


def _merge_kernel(x_ref, ya_ref, att_ref, ga_ref, gb_ref, wa_ref, wb_ref, wo_ref, g2_ref,
                  x1_ref, h2t_ref):
    ya = jnp.dot(ya_ref[...], wa_ref[...], preferred_element_type=F32)
    yb = jnp.dot(att_ref[...], wb_ref[...], preferred_element_type=F32)
    merged = jax.nn.sigmoid(ga_ref[...]) * ya + jax.nn.sigmoid(gb_ref[...]) * yb
    x1 = x_ref[...] + jnp.dot(merged.astype(BF16), wo_ref[...], preferred_element_type=F32)
    x1_ref[...] = x1
    ms = jnp.mean(x1 * x1, axis=-1, keepdims=True)
    h2 = x1 * lax.rsqrt(ms + NORM_EPS) * g2_ref[...]
    h2t_ref[...] = h2.T.astype(BF16)


def _merge(x2d, ya_in, att, proj, wa, wb, wo, g2, tm=256):
    T, D = x2d.shape
    full = lambda a: pl.BlockSpec(a.shape, lambda i: (0,) * a.ndim)
    return pl.pallas_call(
        _merge_kernel,
        grid=(T // tm,),
        in_specs=[
            pl.BlockSpec((tm, D), lambda i: (i, 0)),
            pl.BlockSpec((tm, ya_in.shape[1]), lambda i: (i, 0)),
            pl.BlockSpec((tm, att.shape[1]), lambda i: (i, 0)),
            pl.BlockSpec((tm, D), lambda i: (i, COL_GA // D)),
            pl.BlockSpec((tm, D), lambda i: (i, COL_GB // D)),
            full(wa), full(wb), full(wo), full(g2),
        ],
        out_specs=[pl.BlockSpec((tm, D), lambda i: (i, 0)),
                   pl.BlockSpec((D, tm), lambda i: (0, i))],
        out_shape=[jax.ShapeDtypeStruct((T, D), F32), jax.ShapeDtypeStruct((D, T), BF16)],
        compiler_params=_cparams(("parallel",)),
        name="merge",
    )(x2d, ya_in, att, proj, proj, wa, wb, wo, g2)


NOT_TOP = 127.0


def _top_values(s, k, with_rank=False):
    vals = []
    cur = s
    rank = jnp.full(s.shape, NOT_TOP, F32)
    for i in range(k):
        m = jnp.max(cur, axis=0, keepdims=True)
        vals.append(m)
        hit = cur == m
        if with_rank:
            rank = jnp.where(hit, float(i), rank)
        if i + 1 < k:
            cur = jnp.where(hit, -BIG, cur)
    return (vals, rank) if with_rank else vals


def _peer_prep_kernel(h2t_ref, wqt_ref, sk_ref, r2_ref, e2_ref, n_ref, w1_ref):
    K = PEER_TOPK
    nk = PEER_N_KEYS
    qt = jnp.dot(wqt_ref[...], h2t_ref[...], preferred_element_type=F32).astype(BF16)
    for h in range(PEER_HEADS):
        s1 = jnp.dot(sk_ref[2 * h], qt[(2 * h) * nk:(2 * h + 1) * nk], preferred_element_type=F32)
        s2 = jnp.dot(sk_ref[2 * h + 1], qt[(2 * h + 1) * nk:(2 * h + 2) * nk],
                     preferred_element_type=F32)
        v1 = _top_values(s1, K)
        v2, rank2 = _top_values(s2, K, with_rank=True)
        V1 = jnp.concatenate(v1, axis=0)
        V2 = jnp.concatenate(v2, axis=0)
        groups = [v1[0] + V2, v1[1] + V2[0:8]]
        groups += [v1[i] + V2[0:8] for i in range(2, 8)]
        groups.append(V1[8:16] + v2[0])
        cand = jnp.concatenate(groups, axis=0)
        tops = _top_values(cand, K)
        tau = tops[K - 1]
        z = jnp.ones_like(tau)
        for t in tops[1:]:
            z = z + jnp.exp(t - tops[0])
        n = jnp.zeros(s1.shape, F32)
        for i in range(K):
            cnt = jnp.sum(jnp.where(v1[i] + V2 >= tau, 1.0, 0.0), axis=0, keepdims=True)
            n = jnp.where(s1 == v1[i], cnt, n)
        rows = slice(h * nk, (h + 1) * nk)
        r2_ref[rows, :] = rank2.astype(BF16)
        e2_ref[rows, :] = jnp.exp(s2 - v2[0]).astype(BF16)
        n_ref[rows, :] = n.astype(BF16)
        w1_ref[rows, :] = (jnp.exp(s1 - v1[0]) / z).astype(BF16)


def _peer_prep(h2t, wqt, sk, tp=256):
    D, T = h2t.shape
    rows = PEER_HEADS * PEER_N_KEYS
    full = lambda a: pl.BlockSpec(a.shape, lambda i: (0,) * a.ndim)
    out_blk = pl.BlockSpec((rows, tp), lambda i: (0, i))
    return pl.pallas_call(
        _peer_prep_kernel,
        grid=(T // tp,),
        in_specs=[pl.BlockSpec((D, tp), lambda i: (0, i)), full(wqt), full(sk)],
        out_specs=[out_blk] * 4,
        out_shape=[jax.ShapeDtypeStruct((rows, T), BF16)] * 4,
        compiler_params=_cparams(("parallel",)),
        name="peer_prep",
    )(h2t, wqt, sk)


BF16_ROWS = 16


def _peer_kernel(h2t_ref, u_ref, vt_ref, r2_ref, e2_ref, n_ref, w1_ref, x1_ref, o_ref,
                 acc_ref, a_ref, ga_ref):
    e = pl.program_id(1)
    et, tl = a_ref.shape
    nk = PEER_N_KEYS
    n_ab = et // nk
    assert n_ab == BF16_ROWS
    zero = jnp.zeros((), BF16)

    @pl.when(e == 0)
    def _():
        acc_ref[...] = jnp.zeros_like(acc_ref)

    a_ref[...] = jnp.dot(u_ref[...], h2t_ref[...], preferred_element_type=F32)

    def lane_chunk(lc, _):
        ls = pl.ds(pl.multiple_of(lc * LANES, LANES), LANES)
        for ab in range(n_ab):
            gsum = None
            for h in range(PEER_HEADS):
                grp = pl.ds(pl.multiple_of(h * nk + e * n_ab, n_ab), n_ab)
                n_row = n_ref[grp, ls][ab:ab + 1]
                w_row = w1_ref[grp, ls][ab:ab + 1]
                rs = slice(h * nk, (h + 1) * nk)
                term = jnp.where(r2_ref[rs, ls] < n_row, e2_ref[rs, ls] * w_row, zero)
                gsum = term if gsum is None else gsum + term
            arows = slice(ab * nk, (ab + 1) * nk)
            x = a_ref[arows, ls]
            act = 0.5 * x * (1.0 + lax.erf(x * (1.0 / math.sqrt(2.0))))
            ga_ref[arows, ls] = gsum * act.astype(BF16)
        return 0

    lax.fori_loop(0, tl // LANES, lane_chunk, 0)
    acc_ref[...] += jnp.dot(vt_ref[...], ga_ref[...], preferred_element_type=F32)

    @pl.when(e == pl.num_programs(1) - 1)
    def _():
        o_ref[...] = x1_ref[...] + acc_ref[...].T


def _peer(h2t, u_b, vt_b, r2, e2, n, w1, x1, tl=512, et=BF16_ROWS * PEER_N_KEYS):
    D, T = h2t.shape
    E = u_b.shape[0]
    rows = PEER_HEADS * PEER_N_KEYS
    tok = pl.BlockSpec((rows, tl), lambda i, e: (0, i))
    return pl.pallas_call(
        _peer_kernel,
        grid=(T // tl, E // et),
        in_specs=[
            pl.BlockSpec((D, tl), lambda i, e: (0, i)),
            pl.BlockSpec((et, D), lambda i, e: (e, 0)),
            pl.BlockSpec((D, et), lambda i, e: (0, e)),
            tok, tok, tok, tok,
            pl.BlockSpec((tl, D), lambda i, e: (i, 0)),
        ],
        out_specs=pl.BlockSpec((tl, D), lambda i, e: (i, 0)),
        out_shape=jax.ShapeDtypeStruct((T, D), F32),
        scratch_shapes=[pltpu.VMEM((D, tl), F32), pltpu.VMEM((et, tl), F32),
                        pltpu.VMEM((et, tl), BF16)],
        compiler_params=_cparams(("parallel", "arbitrary")),
        name="peer",
    )(h2t, u_b, vt_b, r2, e2, n, w1, x1)


def _layer(x, mix_g, w_in, ret_w, q_gain, k_gain, moba_w, rel_bias, w_out, ffn_g,
           peer_wq, peer_sk, peer_u, peer_v):
    B, S, D = x.shape
    T = B * S
    x2d = x.reshape(T, D)
    w_rot = jnp.concatenate([w_in[:, ORIG_GATE_START:], w_in[:, :ORIG_GATE_START]],
                            axis=1).astype(BF16)
    proj = _proj(x2d, mix_g.reshape(1, D), w_rot)
    proj3 = proj.reshape(B, S, IN_WIDTH)

    ya_in = _retention(proj3)

    qg = jnp.tile(q_gain, MOBA_HEADS).reshape(1, -1)
    kg = jnp.tile(k_gain, MOBA_HEADS).reshape(1, -1)
    qn, kn, vt, kmean = _moba_prep(proj3, qg, kg)
    att = _moba(qn, kn, vt, kmean.reshape(B, S // MOBA_BLOCK, -1), _bias_tiles(rel_bias))

    x1, h2t = _merge(x2d, ya_in.reshape(T, -1), att.reshape(T, -1), proj,
                     ret_w.astype(BF16), moba_w.astype(BF16), w_out.astype(BF16),
                     ffn_g.reshape(1, D))

    sk = peer_sk.reshape(2 * PEER_HEADS, PEER_N_KEYS, PEER_QUERY_DIM // 2).astype(BF16)
    r2, e2, n, w1 = _peer_prep(h2t, peer_wq.T.astype(BF16), sk)
    out = _peer(h2t, peer_u.astype(BF16), peer_v.T.astype(BF16), r2, e2, n, w1, x1)
    return out.reshape(B, S, D)


def kernel(x, mix_norm_g, w_in, ret_w_branch, moba_q_gain, moba_k_gain, moba_w_branch,
           rel_bias, w_out, ffn_norm_g, peer_w_q, peer_sub_keys, peer_u, peer_v):
    for l in range(mix_norm_g.shape[0]):
        x = _layer(x, mix_norm_g[l], w_in[l], ret_w_branch[l], moba_q_gain[l], moba_k_gain[l],
                   moba_w_branch[l], rel_bias, w_out[l], ffn_norm_g[l], peer_w_q[l],
                   peer_sub_keys[l], peer_u[l], peer_v[l])
    return x
```

```python
import functools
import math

import numpy as np
import jax
import jax.numpy as jnp
from jax import lax
from jax.experimental import pallas as pl
from jax.experimental.pallas import tpu as pltpu

F32 = jnp.float32
BF16 = jnp.bfloat16

D_MODEL = 1024
RET_HEADS = 8
RET_QK_DIM = 64
RET_V_DIM = 128
ROPE_BASE = 10000.0
MOBA_HEADS = 8
MOBA_HEAD_DIM = 64
MOBA_BLOCK = 256
MOBA_TOPK = 3
REL_BUCKETS = 32
REL_MAX_DIST = 2048
PEER_HEADS = 8
PEER_N_KEYS = 128
PEER_N_EXPERTS = PEER_N_KEYS * PEER_N_KEYS
PEER_QUERY_DIM = 256
PEER_TOPK = 16
NORM_EPS = 1e-6
NEG_INF = -1e30
BIG = 3.0e38

LANES = 128
VMEM_LIMIT = 56 * 1024 * 1024

COL_GA, COL_GB = 0, 1024
COL_RQ, COL_RK, COL_RV, COL_RG = 2048, 2560, 3072, 4096
COL_MQ, COL_MK, COL_MV = 5120, 5632, 6144
IN_WIDTH = 6656
ORIG_GATE_START = 4608

RET_CHUNK = 256
N_BIAS_TILES = 8


def _cparams(sem):
    return pltpu.CompilerParams(dimension_semantics=sem, vmem_limit_bytes=VMEM_LIMIT)


def _proj_kernel(x_ref, g_ref, w_ref, o_ref):
    x = x_ref[...]
    ms = jnp.mean(x * x, axis=-1, keepdims=True)
    h = (x * lax.rsqrt(ms + NORM_EPS) * g_ref[...]).astype(BF16)
    o_ref[...] = jnp.dot(h, w_ref[...], preferred_element_type=F32)


def _proj(x2d, g, w_bf16, tm=256):
    T, D = x2d.shape
    N = w_bf16.shape[1]
    return pl.pallas_call(
        _proj_kernel,
        grid=(T // tm,),
        in_specs=[
            pl.BlockSpec((tm, D), lambda i: (i, 0)),
            pl.BlockSpec((1, D), lambda i: (0, 0)),
            pl.BlockSpec((D, N), lambda i: (0, 0)),
        ],
        out_specs=pl.BlockSpec((tm, N), lambda i: (i, 0)),
        out_shape=jax.ShapeDtypeStruct((T, N), F32),
        compiler_params=_cparams(("parallel",)),
        name="proj",
    )(x2d, g, w_bf16)


def _rope(x, cos, sin_signed, first_half):
    swapped = jnp.where(first_half, pltpu.roll(x, LANES - 32, 1), pltpu.roll(x, 32, 1))
    return x * cos + swapped * sin_signed


def _retention_kernel(q_ref, k_ref, v_ref, rg_ref, cos_ref, sin_ref, dmat_ref,
                      kdec_ref, qdec_ref, cdm_ref, bmask_ref, o_ref, state_ref):
    @pl.when(pl.program_id(2) == 0)
    def _():
        state_ref[...] = jnp.zeros_like(state_ref)

    C = q_ref.shape[1]
    lane = lax.broadcasted_iota(jnp.int32, (C, LANES), 1)
    first_half = (lane % RET_QK_DIM) < (RET_QK_DIM // 2)
    cos = cos_ref[...]
    sin = sin_ref[...]
    qr = _rope(q_ref[0], cos, sin, first_half)
    kr = _rope(k_ref[0], cos, sin, first_half) * (RET_QK_DIM ** -0.5)
    kr_b = kr.astype(BF16)
    v_b = v_ref[0].astype(BF16)
    state = state_ref[...]

    o_cross = jnp.dot(qr.astype(BF16), state.astype(BF16),
                      preferred_element_type=F32) * qdec_ref[0]
    rg = rg_ref[0]
    for h in range(2):
        qh = jnp.where((lane // RET_QK_DIM) == h, qr, 0.0).astype(BF16)
        s = lax.dot_general(qh, kr_b, (((1,), (1,)), ((), ())),
                            preferred_element_type=F32)
        s = s * dmat_ref[h]
        sl = slice(h * RET_V_DIM, (h + 1) * RET_V_DIM)
        o = jnp.dot(s.astype(BF16), v_b[:, sl], preferred_element_type=F32) + o_cross[:, sl]
        mu = jnp.mean(o, axis=-1, keepdims=True)
        d = o - mu
        var = jnp.mean(d * d, axis=-1, keepdims=True)
        y = d * lax.rsqrt(var + NORM_EPS)
        g = rg[:, sl]
        o_ref[0, :, sl] = (y * (g * jax.nn.sigmoid(g))).astype(BF16)

    kd = (kr * kdec_ref[0]).astype(BF16)
    kv = lax.dot_general(kd, v_b, (((0,), (0,)), ((), ())), preferred_element_type=F32)
    state_ref[...] = state * cdm_ref[0] + kv * bmask_ref[...]


def _retention_tables(S, C):
    H = RET_HEADS
    log_g = jnp.log1p(-jnp.exp2(-5.0 - jnp.arange(H, dtype=F32)))
    idx = jnp.arange(C, dtype=F32)
    rel = idx[:, None] - idx[None, :]
    dmat = jnp.where(rel[None] >= 0,
                     jnp.exp(jnp.maximum(rel, 0.0)[None] * log_g[:, None, None]), 0.0)
    kdec = jnp.exp((C - 1 - idx)[None, :] * log_g[:, None])
    qdec = jnp.exp((idx + 1.0)[None, :] * log_g[:, None])
    cd = jnp.exp(C * log_g)
    kdec_pair = jnp.repeat(kdec.reshape(H // 2, 2, C).transpose(0, 2, 1), RET_QK_DIM, axis=2)
    qdec_pair = jnp.repeat(qdec.reshape(H // 2, 2, C).transpose(0, 2, 1), RET_V_DIM, axis=2)
    rows = np.arange(2 * RET_QK_DIM)[:, None] // RET_QK_DIM
    cols = np.arange(2 * RET_V_DIM)[None, :] // RET_V_DIM
    bmask = jnp.asarray((rows == cols).astype(np.float32))
    cd_rows = jnp.repeat(cd.reshape(H // 2, 2), RET_QK_DIM, axis=1)
    cdm = cd_rows[:, :, None] * bmask[None]
    half = RET_QK_DIM // 2
    inv = ROPE_BASE ** (-jnp.arange(half, dtype=F32) / half)
    ang = jnp.arange(S).astype(F32)[:, None] * inv[None, :]
    cos, sin = jnp.cos(ang), jnp.sin(ang)
    cos_t = jnp.tile(cos, (1, LANES // half))
    sin_t = jnp.tile(jnp.concatenate([-sin, sin], axis=1), (1, LANES // RET_QK_DIM))
    return dmat, kdec_pair, qdec_pair, cdm, bmask, cos_t, sin_t


def _retention(proj3, C=RET_CHUNK):
    B, S, _ = proj3.shape
    dmat, kdec, qdec, cdm, bmask, cos_t, sin_t = _retention_tables(S, C)
    n_pairs = RET_HEADS // 2
    return pl.pallas_call(
        _retention_kernel,
        grid=(B, n_pairs, S // C),
        in_specs=[
            pl.BlockSpec((1, C, LANES), lambda b, p, c: (b, c, COL_RQ // LANES + p)),
            pl.BlockSpec((1, C, LANES), lambda b, p, c: (b, c, COL_RK // LANES + p)),
            pl.BlockSpec((1, C, 256), lambda b, p, c: (b, c, COL_RV // 256 + p)),
            pl.BlockSpec((1, C, 256), lambda b, p, c: (b, c, COL_RG // 256 + p)),
            pl.BlockSpec((C, LANES), lambda b, p, c: (c, 0)),
            pl.BlockSpec((C, LANES), lambda b, p, c: (c, 0)),
            pl.BlockSpec((2, C, C), lambda b, p, c: (p, 0, 0)),
            pl.BlockSpec((1, C, LANES), lambda b, p, c: (p, 0, 0)),
            pl.BlockSpec((1, C, 256), lambda b, p, c: (p, 0, 0)),
            pl.BlockSpec((1, LANES, 256), lambda b, p, c: (p, 0, 0)),
            pl.BlockSpec((LANES, 256), lambda b, p, c: (0, 0)),
        ],
        out_specs=pl.BlockSpec((1, C, 256), lambda b, p, c: (b, c, p)),
        out_shape=jax.ShapeDtypeStruct((B, S, RET_HEADS * RET_V_DIM), BF16),
        scratch_shapes=[pltpu.VMEM((LANES, 256), F32)],
        compiler_params=_cparams(("parallel", "parallel", "arbitrary")),
        name="retention",
    )(proj3, proj3, proj3, proj3, cos_t, sin_t, dmat, kdec, qdec, cdm, bmask)


def _head_rmsnorm(x, gain):
    R = x.shape[0]
    lane = lax.broadcasted_iota(jnp.int32, (R, LANES), 1)
    lo = lane < MOBA_HEAD_DIM
    outs = []
    for p in range(x.shape[1] // LANES):
        xs = x[:, p * LANES:(p + 1) * LANES]
        x2 = xs * xs
        m0 = jnp.sum(jnp.where(lo, x2, 0.0), axis=-1, keepdims=True) * (1.0 / MOBA_HEAD_DIM)
        m1 = jnp.sum(jnp.where(lo, 0.0, x2), axis=-1, keepdims=True) * (1.0 / MOBA_HEAD_DIM)
        r = jnp.where(lo, lax.rsqrt(m0 + NORM_EPS), lax.rsqrt(m1 + NORM_EPS))
        outs.append(xs * r * gain[:, p * LANES:(p + 1) * LANES])
    return jnp.concatenate(outs, axis=1)


def _moba_prep_kernel(q_ref, k_ref, v_ref, qg_ref, kg_ref, qn_ref, kn_ref, vt_ref, km_ref):
    qn = _head_rmsnorm(q_ref[0], qg_ref[...]) * (MOBA_HEAD_DIM ** -0.5)
    kn = _head_rmsnorm(k_ref[0], kg_ref[...])
    qn_ref[0] = qn.astype(BF16)
    kn_ref[0] = kn.astype(BF16)
    vt_ref[0, 0] = v_ref[0].T.astype(BF16)
    km_ref[0, 0] = jnp.mean(kn, axis=0, keepdims=True)


def _moba_prep(proj3, qg, kg):
    B, S, _ = proj3.shape
    L = MOBA_BLOCK
    W = MOBA_HEADS * MOBA_HEAD_DIM
    nb = S // L
    blk = lambda col: pl.BlockSpec((1, L, W), lambda b, s: (b, s, col // W))
    out_blk = pl.BlockSpec((1, L, W), lambda b, s: (b, s, 0))
    return pl.pallas_call(
        _moba_prep_kernel,
        grid=(B, nb),
        in_specs=[blk(COL_MQ), blk(COL_MK), blk(COL_MV),
                  pl.BlockSpec((1, W), lambda b, s: (0, 0)),
                  pl.BlockSpec((1, W), lambda b, s: (0, 0))],
        out_specs=[out_blk, out_blk,
                   pl.BlockSpec((1, 1, W, L), lambda b, s: (b, s, 0, 0)),
                   pl.BlockSpec((1, 1, 1, W), lambda b, s: (b, s, 0, 0))],
        out_shape=[jax.ShapeDtypeStruct((B, S, W), BF16)] * 2
        + [jax.ShapeDtypeStruct((B, nb, W, L), BF16),
           jax.ShapeDtypeStruct((B, nb, 1, W), F32)],
        compiler_params=_cparams(("parallel", "parallel")),
        name="moba_prep",
    )(proj3, proj3, proj3, qg, kg)


def _moba_kernel(q_ref, k_ref, vt_ref, km_ref, bias_ref, o_ref, sel_ref, s_ref, acc_ref):
    L = MOBA_BLOCK
    qi = pl.program_id(2)
    nb = km_ref.shape[1]
    q = q_ref[0]
    lane = lax.broadcasted_iota(jnp.int32, (L, LANES), 1)
    km = km_ref[0].astype(BF16)
    blk = lax.broadcasted_iota(jnp.int32, (nb, L), 0)
    past = blk < qi
    key_row = lax.broadcasted_iota(jnp.int32, (L, L), 0)
    q_col = lax.broadcasted_iota(jnp.int32, (L, L), 1)
    causal = key_row <= q_col
    nt = (((1,), (1,)), ((), ()))
    qhs = []
    for h in range(2):
        qh = jnp.where((lane // MOBA_HEAD_DIM) == h, q, jnp.zeros_like(q))
        qhs.append(qh)
        g = lax.dot_general(km, qh, nt, preferred_element_type=F32)
        g = jnp.where(past, g, NEG_INF)
        cnt = jnp.zeros((nb, L), F32)
        for m in range(nb):
            gm = g[m:m + 1, :]
            beats = jnp.where(gm > g, 1.0, jnp.where(gm == g, jnp.where(blk > m, 1.0, 0.0), 0.0))
            cnt = cnt + beats
        sel = jnp.where(cnt < MOBA_TOPK, jnp.where(past, 1.0, 0.0), 0.0)
        for j in range(nb):
            sel_ref[h, j] = jnp.broadcast_to(sel[j:j + 1, :], (8, L))

    def scores(h, j, bias_idx):
        kb = k_ref[0, pl.ds(pl.multiple_of(j * L, L), L), :]
        s = lax.dot_general(kb, qhs[h], nt, preferred_element_type=F32)
        return s + bias_ref[h, bias_idx]

    def fold(x, op):
        return op(x.reshape(L // 8, 8, L), axis=0)

    mx = []
    for h in range(2):
        s = jnp.where(causal, scores(h, qi, 0), NEG_INF)
        s_ref[h, qi] = s
        mx.append(fold(s, jnp.max))

    def pass1(t, mx):
        out = list(mx)
        for j in (2 * t, jnp.minimum(2 * t + 1, qi - 1)):
            bias_idx = jnp.minimum(qi - j, N_BIAS_TILES - 1)
            for h in range(2):
                s = jnp.where(sel_ref[h, j][0:1, :] > 0.5, scores(h, j, bias_idx), NEG_INF)
                s_ref[h, j] = s
                out[h] = jnp.maximum(out[h], fold(s, jnp.max))
        return tuple(out)

    mx = lax.fori_loop(0, (qi + 1) // 2, pass1, tuple(mx))
    m = [jnp.max(x, axis=0, keepdims=True) for x in mx]

    acc_ref[...] = jnp.zeros_like(acc_ref)
    for h in range(2):
        s_ref[h, qi + 1] = jnp.full((L, L), NEG_INF, F32)

    def pass2(t, ls):
        out = list(ls)
        for j in (2 * t, 2 * t + 1):
            vt = vt_ref[0, jnp.minimum(j, nb - 1)]
            for h in range(2):
                p = jnp.exp(s_ref[h, j] - m[h])
                out[h] = out[h] + fold(p, jnp.sum)
                acc_ref[h] += jnp.dot(vt, p.astype(BF16), preferred_element_type=F32)
        return tuple(out)

    zeros = jnp.zeros((8, L), F32)
    ls = lax.fori_loop(0, (qi + 2) // 2, pass2, (zeros, zeros))
    l = [jnp.sum(x, axis=0, keepdims=True) for x in ls]
    row = lax.broadcasted_iota(jnp.int32, (LANES, L), 0)
    ot = jnp.where(row < MOBA_HEAD_DIM, acc_ref[0] / l[0], acc_ref[1] / l[1])
    o_ref[0] = ot.T.astype(BF16)


def _rel_bucket(dist):
    max_exact = REL_BUCKETS // 2
    n = jnp.maximum(dist, 0)
    nf = jnp.maximum(n, 1).astype(F32)
    large = max_exact + jnp.floor(jnp.log(nf / max_exact) / math.log(REL_MAX_DIST / max_exact)
                                  * (REL_BUCKETS - max_exact)).astype(jnp.int32)
    large = jnp.minimum(large, REL_BUCKETS - 1)
    return jnp.where(n < max_exact, n, large)


def _bias_kernel(rb_ref, bucket_ref, o_ref):
    h = pl.program_id(0)
    b = bucket_ref[...]
    acc = jnp.zeros(b.shape, F32)
    for k in range(REL_BUCKETS):
        acc = jnp.where(b == k, rb_ref[k, h], acc)
    o_ref[0] = acc


def _bias_tiles(rel_bias):
    L = MOBA_BLOCK
    r = np.arange(L)[None, :] - np.arange(L)[:, None]
    dist = np.stack([d * L + r for d in range(N_BIAS_TILES)]).astype(np.int32)
    n_sat = (N_BIAS_TILES - 1) * L - (L - 1)
    sat = REL_BUCKETS // 2 + int(math.log(n_sat / (REL_BUCKETS // 2))
                                 / math.log(REL_MAX_DIST / (REL_BUCKETS // 2))
                                 * (REL_BUCKETS - REL_BUCKETS // 2) * (1 - 1e-6))
    assert sat >= REL_BUCKETS - 1
    bucket = _rel_bucket(jnp.asarray(dist))
    return pl.pallas_call(
        _bias_kernel,
        grid=(MOBA_HEADS,),
        in_specs=[pl.BlockSpec(memory_space=pltpu.SMEM),
                  pl.BlockSpec(bucket.shape, lambda h: (0, 0, 0))],
        out_specs=pl.BlockSpec((1,) + bucket.shape, lambda h: (h, 0, 0, 0)),
        out_shape=jax.ShapeDtypeStruct((MOBA_HEADS,) + bucket.shape, F32),
        compiler_params=_cparams(("parallel",)),
        name="rel_bias_tiles",
    )(rel_bias.astype(F32), bucket)


def _moba(qn, kn, vt, kmean, bias_tiles):
    B, S, W = qn.shape
    L = MOBA_BLOCK
    nb = S // L
    n_pairs = MOBA_HEADS // 2
    return pl.pallas_call(
        _moba_kernel,
        grid=(B, n_pairs, nb),
        in_specs=[
            pl.BlockSpec((1, L, LANES), lambda b, p, i: (b, i, p)),
            pl.BlockSpec((1, S, LANES), lambda b, p, i: (b, 0, p)),
            pl.BlockSpec((1, nb, LANES, L), lambda b, p, i: (b, 0, p, 0)),
            pl.BlockSpec((1, nb, LANES), lambda b, p, i: (b, 0, p)),
            pl.BlockSpec((2, N_BIAS_TILES, L, L), lambda b, p, i: (p, 0, 0, 0)),
        ],
        out_specs=pl.BlockSpec((1, L, LANES), lambda b, p, i: (b, i, p)),
        out_shape=jax.ShapeDtypeStruct((B, S, W), BF16),
        scratch_shapes=[pltpu.VMEM((2, nb, 8, L), F32), pltpu.VMEM((2, nb + 1, L, L), F32),
                        pltpu.VMEM((2, LANES, L), F32)],
        compiler_params=_cparams(("parallel", "parallel", "arbitrary")),
        name="moba",
    )(qn, kn, vt, kmean, bias_tiles)


def _merge_kernel(x_ref, ya_ref, att_ref, ga_ref, gb_ref, wa_ref, wb_ref, wo_ref, g2_ref,
                  x1_ref, h2t_ref):
    ya = jnp.dot(ya_ref[...], wa_ref[...], preferred_element_type=F32)
    yb = jnp.dot(att_ref[...], wb_ref[...], preferred_element_type=F32)
    merged = jax.nn.sigmoid(ga_ref[...]) * ya + jax.nn.sigmoid(gb_ref[...]) * yb
    x1 = x_ref[...] + jnp.dot(merged.astype(BF16), wo_ref[...], preferred_element_type=F32)
    x1_ref[...] = x1
    ms = jnp.mean(x1 * x1, axis=-1, keepdims=True)
    h2 = x1 * lax.rsqrt(ms + NORM_EPS) * g2_ref[...]
    h2t_ref[...] = h2.T.astype(BF16)


def _merge(x2d, ya_in, att, proj, wa, wb, wo, g2, tm=256):
    T, D = x2d.shape
    full = lambda a: pl.BlockSpec(a.shape, lambda i: (0,) * a.ndim)
    return pl.pallas_call(
        _merge_kernel,
        grid=(T // tm,),
        in_specs=[
            pl.BlockSpec((tm, D), lambda i: (i, 0)),
            pl.BlockSpec((tm, ya_in.shape[1]), lambda i: (i, 0)),
            pl.BlockSpec((tm, att.shape[1]), lambda i: (i, 0)),
            pl.BlockSpec((tm, D), lambda i: (i, COL_GA // D)),
            pl.BlockSpec((tm, D), lambda i: (i, COL_GB // D)),
            full(wa), full(wb), full(wo), full(g2),
        ],
        out_specs=[pl.BlockSpec((tm, D), lambda i: (i, 0)),
                   pl.BlockSpec((D, tm), lambda i: (0, i))],
        out_shape=[jax.ShapeDtypeStruct((T, D), F32), jax.ShapeDtypeStruct((D, T), BF16)],
        compiler_params=_cparams(("parallel",)),
        name="merge",
    )(x2d, ya_in, att, proj, proj, wa, wb, wo, g2)


NOT_TOP = 127.0


def _top_values(s, k, with_rank=False):
    vals = []
    cur = s
    rank = jnp.full(s.shape, NOT_TOP, F32)
    for i in range(k):
        m = jnp.max(cur, axis=0, keepdims=True)
        vals.append(m)
        hit = cur == m
        if with_rank:
            rank = jnp.where(hit, float(i), rank)
        if i + 1 < k:
            cur = jnp.where(hit, -BIG, cur)
    return (vals, rank) if with_rank else vals


def _peer_prep_kernel(h2t_ref, wqt_ref, sk_ref, r2_ref, e2_ref, n_ref, w1_ref):
    K = PEER_TOPK
    nk = PEER_N_KEYS
    qt = jnp.dot(wqt_ref[...], h2t_ref[...], preferred_element_type=F32).astype(BF16)
    for h in range(PEER_HEADS):
        s1 = jnp.dot(sk_ref[2 * h], qt[(2 * h) * nk:(2 * h + 1) * nk], preferred_element_type=F32)
        s2 = jnp.dot(sk_ref[2 * h + 1], qt[(2 * h + 1) * nk:(2 * h + 2) * nk],
                     preferred_element_type=F32)
        v1 = _top_values(s1, K)
        v2, rank2 = _top_values(s2, K, with_rank=True)
        V1 = jnp.concatenate(v1, axis=0)
        V2 = jnp.concatenate(v2, axis=0)
        groups = [v1[0] + V2, v1[1] + V2[0:8]]
        groups += [v1[i] + V2[0:8] for i in range(2, 8)]
        groups.append(V1[8:16] + v2[0])
        cand = jnp.concatenate(groups, axis=0)
        tops = _top_values(cand, K)
        tau = tops[K - 1]
        z = jnp.ones_like(tau)
        for t in tops[1:]:
            z = z + jnp.exp(t - tops[0])
        n = jnp.zeros(s1.shape, F32)
        for i in range(K):
            cnt = jnp.sum(jnp.where(v1[i] + V2 >= tau, 1.0, 0.0), axis=0, keepdims=True)
            n = jnp.where(s1 == v1[i], cnt, n)
        rows = slice(h * nk, (h + 1) * nk)
        prow = slice(h * nk // 2, (h + 1) * nk // 2)
        r2p = pltpu.bitcast(rank2.astype(BF16), jnp.uint32)
        e2p = pltpu.bitcast(jnp.exp(s2 - v2[0]).astype(BF16), jnp.uint32)
        for c in range(r2_ref.shape[0]):
            r2_ref[c, prow, :] = r2p[:, c * LANES:(c + 1) * LANES]
            e2_ref[c, prow, :] = e2p[:, c * LANES:(c + 1) * LANES]
        n_ref[rows, :] = n.astype(BF16)
        w1_ref[rows, :] = (jnp.exp(s1 - v1[0]) / z).astype(BF16)


def _peer_prep(h2t, wqt, sk, tp=256):
    D, T = h2t.shape
    rows = PEER_HEADS * PEER_N_KEYS
    full = lambda a: pl.BlockSpec(a.shape, lambda i: (0,) * a.ndim)
    out_blk = pl.BlockSpec((rows, tp), lambda i: (0, i))
    pair_blk = pl.BlockSpec((tp // LANES, rows // 2, LANES), lambda i: (i, 0, 0))
    return pl.pallas_call(
        _peer_prep_kernel,
        grid=(T // tp,),
        in_specs=[pl.BlockSpec((D, tp), lambda i: (0, i)), full(wqt), full(sk)],
        out_specs=[pair_blk, pair_blk, out_blk, out_blk],
        out_shape=[jax.ShapeDtypeStruct((T // LANES, rows // 2, LANES), jnp.uint32)] * 2
        + [jax.ShapeDtypeStruct((rows, T), BF16)] * 2,
        compiler_params=_cparams(("parallel",)),
        name="peer_prep",
    )(h2t, wqt, sk)


BF16_ROWS = 16


def _peer_kernel(h2t_ref, u_ref, vt_ref, r2_ref, e2_ref, n_ref, w1_ref, x1_ref, o_ref,
                 acc_ref, a_ref, ga_ref):
    e = pl.program_id(1)
    n_lc, et, _ = a_ref.shape
    nk = PEER_N_KEYS
    n_ab = et // nk
    assert n_ab == BF16_ROWS
    zero = jnp.zeros((), BF16)

    @pl.when(e == 0)
    def _():
        acc_ref[...] = jnp.zeros_like(acc_ref)

    a = jnp.dot(u_ref[...], h2t_ref[...], preferred_element_type=F32)
    for c in range(n_lc):
        a_ref[c] = a[:, c * LANES:(c + 1) * LANES]

    def lane_chunk(lc, _):
        ls = pl.ds(pl.multiple_of(lc * LANES, LANES), LANES)
        for ab in range(n_ab):
            gsum = None
            for h in range(PEER_HEADS):
                grp = pl.ds(pl.multiple_of(h * nk + e * n_ab, n_ab), n_ab)
                n_row = n_ref[grp, ls][ab:ab + 1]
                w_row = w1_ref[grp, ls][ab:ab + 1]
                prs = slice(h * nk // 2, (h + 1) * nk // 2)
                r2 = pltpu.bitcast(r2_ref[lc, prs, :], BF16)
                e2 = pltpu.bitcast(e2_ref[lc, prs, :], BF16)
                term = jnp.where(r2 < n_row, e2 * w_row, zero)
                gsum = term if gsum is None else gsum + term
            x = a_ref[lc, ab * nk:(ab + 1) * nk, :]
            act = 0.5 * x * (1.0 + lax.erf(x * (1.0 / math.sqrt(2.0))))
            ga_ref[ab * nk // 2:(ab + 1) * nk // 2, ls] = pltpu.bitcast(
                gsum * act.astype(BF16), jnp.uint32)
        return 0

    lax.fori_loop(0, n_lc, lane_chunk, 0)
    ga = pltpu.bitcast(ga_ref[...], BF16)
    acc_ref[...] += jnp.dot(vt_ref[...], ga, preferred_element_type=F32)

    @pl.when(e == pl.num_programs(1) - 1)
    def _():
        o_ref[...] = x1_ref[...] + acc_ref[...].T


def _peer(h2t, u_b, vt_b, r2, e2, n, w1, x1, tl=512, et=BF16_ROWS * PEER_N_KEYS):
    D, T = h2t.shape
    E = u_b.shape[0]
    rows = PEER_HEADS * PEER_N_KEYS
    tok = pl.BlockSpec((rows, tl), lambda i, e: (0, i))
    tok_pair = pl.BlockSpec((tl // LANES, rows // 2, LANES), lambda i, e: (i, 0, 0))
    return pl.pallas_call(
        _peer_kernel,
        grid=(T // tl, E // et),
        in_specs=[
            pl.BlockSpec((D, tl), lambda i, e: (0, i)),
            pl.BlockSpec((et, D), lambda i, e: (e, 0)),
            pl.BlockSpec((D, et), lambda i, e: (0, e)),
            tok_pair, tok_pair, tok, tok,
            pl.BlockSpec((tl, D), lambda i, e: (i, 0)),
        ],
        out_specs=pl.BlockSpec((tl, D), lambda i, e: (i, 0)),
        out_shape=jax.ShapeDtypeStruct((T, D), F32),
        scratch_shapes=[pltpu.VMEM((D, tl), F32), pltpu.VMEM((tl // LANES, et, LANES), F32),
                        pltpu.VMEM((et // 2, tl), jnp.uint32)],
        compiler_params=_cparams(("parallel", "arbitrary")),
        name="peer",
    )(h2t, u_b, vt_b, r2, e2, n, w1, x1)


def _layer(x, mix_g, w_in, ret_w, q_gain, k_gain, moba_w, rel_bias, w_out, ffn_g,
           peer_wq, peer_sk, peer_u, peer_v):
    B, S, D = x.shape
    T = B * S
    x2d = x.reshape(T, D)
    w_rot = jnp.concatenate([w_in[:, ORIG_GATE_START:], w_in[:, :ORIG_GATE_START]],
                            axis=1).astype(BF16)
    proj = _proj(x2d, mix_g.reshape(1, D), w_rot)
    proj3 = proj.reshape(B, S, IN_WIDTH)

    ya_in = _retention(proj3)

    qg = jnp.tile(q_gain, MOBA_HEADS).reshape(1, -1)
    kg = jnp.tile(k_gain, MOBA_HEADS).reshape(1, -1)
    qn, kn, vt, kmean = _moba_prep(proj3, qg, kg)
    att = _moba(qn, kn, vt, kmean.reshape(B, S // MOBA_BLOCK, -1), _bias_tiles(rel_bias))

    x1, h2t = _merge(x2d, ya_in.reshape(T, -1), att.reshape(T, -1), proj,
                     ret_w.astype(BF16), moba_w.astype(BF16), w_out.astype(BF16),
                     ffn_g.reshape(1, D))

    sk = peer_sk.reshape(2 * PEER_HEADS, PEER_N_KEYS, PEER_QUERY_DIM // 2).astype(BF16)
    r2, e2, n, w1 = _peer_prep(h2t, peer_wq.T.astype(BF16), sk)
    out = _peer(h2t, peer_u.astype(BF16), peer_v.T.astype(BF16), r2, e2, n, w1, x1)
    return out.reshape(B, S, D)


def kernel(x, mix_norm_g, w_in, ret_w_branch, moba_q_gain, moba_k_gain, moba_w_branch,
           rel_bias, w_out, ffn_norm_g, peer_w_q, peer_sub_keys, peer_u, peer_v):
    for l in range(mix_norm_g.shape[0]):
        x = _layer(x, mix_norm_g[l], w_in[l], ret_w_branch[l], moba_q_gain[l], moba_k_gain[l],
                   moba_w_branch[l], rel_bias, w_out[l], ffn_norm_g[l], peer_w_q[l],
                   peer_sub_keys[l], peer_u[l], peer_v[l])
    return x
```
